```python
import jax, jax.numpy as jnp
from jax import lax
import numpy as np

D_MODEL = 2048
BATCH = 1
SEQ = 16384
DEPTH = 2

D_A = 1024
CONV_A = 3
D_B = 1024
CONV_B = 31
N_HEADS_C = 4
D_C = 2048
HEAD_DIM_C = D_C // N_HEADS_C
CONV_C = 4
CHUNK = 64
D_FF = 5504
N_BRANCH = 3
EPS = 1e-6

SPLIT_SIZES = (D_A, D_A, D_A, D_B, D_B, D_C, D_C, D_C, D_C, N_HEADS_C, N_HEADS_C)
N_IN = sum(SPLIT_SIZES)

kernel_name = "hybrid_gatedconv_conformer_mlstm_macaron"


def _rmsnorm(x, g):
    xf = x.astype(jnp.float32)
    y = xf * lax.rsqrt(jnp.mean(xf * xf, axis=-1, keepdims=True) + EPS)
    return (y * g.astype(jnp.float32)).astype(x.dtype)


def _layernorm(x, g, b):
    xf = x.astype(jnp.float32)
    mu = jnp.mean(xf, axis=-1, keepdims=True)
    xc = xf - mu
    y = xc * lax.rsqrt(jnp.mean(xc * xc, axis=-1, keepdims=True) + EPS)
    return (y * g.astype(jnp.float32) + b.astype(jnp.float32)).astype(x.dtype)


def _causal_dwconv(x, w):
    k_width, ch = w.shape
    xp = jnp.pad(x, ((0, 0), (k_width - 1, 0), (0, 0)))
    return lax.conv_general_dilated(
        xp, w[:, None, :].astype(x.dtype), window_strides=(1,), padding="VALID",
        dimension_numbers=("NWC", "WIO", "NWC"), feature_group_count=ch)


def _swiglu(x, w1, w2):
    gate, up = jnp.split(x @ w1, 2, axis=-1)
    return (jax.nn.silu(gate) * up) @ w2


def _mlstm_chunkwise(q, k, v, ig, lf):
    bsz, s_len, n_h, d_h = q.shape
    nc = s_len // CHUNK

    def to_chunks(t):
        t = t.reshape((bsz, nc, CHUNK) + t.shape[2:])
        return jnp.moveaxis(jnp.moveaxis(t, 1, 0), 3, 2)

    tri = jnp.tril(jnp.ones((CHUNK, CHUNK), dtype=bool))

    def step(carry, inp):
        c_state, n_state, m_state = carry
        qc, kc, vc, ic, fc = inp
        b = jnp.cumsum(fc, axis=-1)
        dmat = jnp.where(tri, b[..., :, None] - b[..., None, :] + ic[..., None, :], -jnp.inf)
        inter = b + m_state[..., None]
        m_t = jnp.maximum(inter, jnp.max(dmat, axis=-1))
        w_inter = jnp.exp(inter - m_t)
        s = jnp.einsum("bhtd,bhsd->bhts", qc, kc) * jnp.exp(dmat - m_t[..., None])
        num = (w_inter[..., None] * jnp.einsum("bhtd,bhde->bhte", qc, c_state)
               + jnp.einsum("bhts,bhse->bhte", s, vc))
        den = w_inter * jnp.einsum("bhtd,bhd->bht", qc, n_state) + jnp.sum(s, axis=-1)
        h = num / jnp.maximum(jnp.abs(den), jnp.exp(-m_t))[..., None]
        m_new = m_t[..., -1]
        decay = jnp.exp(b[..., -1] + m_state - m_new)
        wk = jnp.exp(b[..., -1:] - b + ic - m_new[..., None])
        c_state = decay[..., None, None] * c_state + jnp.einsum("bhs,bhsd,bhse->bhde", wk, kc, vc)
        n_state = decay[..., None] * n_state + jnp.einsum("bhs,bhsd->bhd", wk, kc)
        return (c_state, n_state, m_new), h

    init = (jnp.zeros((bsz, n_h, d_h, d_h), jnp.float32),
            jnp.zeros((bsz, n_h, d_h), jnp.float32),
            jnp.zeros((bsz, n_h), jnp.float32))
    xs = (to_chunks(q), to_chunks(k), to_chunks(v), to_chunks(ig), to_chunks(lf))
    _, hs = lax.scan(step, init, xs)
    hs = jnp.moveaxis(jnp.moveaxis(hs, 0, 1), 2, 3)
    return hs.reshape(bsz, s_len, n_h, d_h)


def _hybrid_mixer(h, w_in, conv_a_w, w_out_a, conv_b_w, conv_b_bias, ln_b_gain, ln_b_bias,
                  w_out_b, conv_c_w, ig_bias, fg_bias, mh_norm, w_out_c, w_gate, b_gate, w_o):
    bsz, s_len, _ = h.shape
    u = h @ w_in
    idx = [int(i) for i in np.cumsum(SPLIT_SIZES)[:-1]]
    a_b, a_c, a_x, b_val, b_gt, c_q, c_k, c_v, c_o, c_i, c_f = jnp.split(u, idx, axis=-1)

    y_a = (a_b * _causal_dwconv(a_c * a_x, conv_a_w)) @ w_out_a

    glu = b_val * jax.nn.sigmoid(b_gt)
    z = _causal_dwconv(glu, conv_b_w) + conv_b_bias
    z = jax.nn.silu(_layernorm(z, ln_b_gain, ln_b_bias))
    y_b = z @ w_out_b

    qk = jax.nn.silu(_causal_dwconv(jnp.concatenate([c_q, c_k], axis=-1), conv_c_w))
    q, k = jnp.split(qk, 2, axis=-1)
    q = q.reshape(bsz, s_len, N_HEADS_C, HEAD_DIM_C).astype(jnp.float32) * (HEAD_DIM_C ** -0.5)
    k = k.reshape(bsz, s_len, N_HEADS_C, HEAD_DIM_C).astype(jnp.float32)
    v = c_v.reshape(bsz, s_len, N_HEADS_C, HEAD_DIM_C).astype(jnp.float32)
    ig = (c_i + ig_bias).astype(jnp.float32)
    lf = jax.nn.log_sigmoid((c_f + fg_bias).astype(jnp.float32))
    h_t = _mlstm_chunkwise(q, k, v, ig, lf)
    h_t = h_t * lax.rsqrt(jnp.mean(h_t * h_t, axis=-1, keepdims=True) + EPS)
    h_t = h_t * mh_norm.reshape(N_HEADS_C, HEAD_DIM_C).astype(jnp.float32)
    h_t = h_t.reshape(bsz, s_len, D_C).astype(h.dtype)
    y_c = (jax.nn.sigmoid(c_o) * h_t) @ w_out_c

    g_a, g_b, g_c = jnp.split(jax.nn.sigmoid(h @ w_gate + b_gate), N_BRANCH, axis=-1)
    return (g_a * y_a + g_b * y_b + g_c * y_c) @ w_o


def setup_inputs(seed: int = 0) -> dict:
    key = jax.random.key(seed)
    ks = jax.random.split(key, 26)
    D, L, H = D_MODEL, DEPTH, N_HEADS_C

    def nrm(k, shape, scale):
        return jax.random.normal(k, shape, jnp.float32) * scale

    return {
        "x": nrm(ks[0], (BATCH, SEQ, D), 1.0),
        "ffn1_norm": 1.0 + nrm(ks[1], (L, D), 0.02),
        "ffn1_w1": nrm(ks[2], (L, D, 2 * D_FF), D ** -0.5),
        "ffn1_w2": nrm(ks[3], (L, D_FF, D), D_FF ** -0.5),
        "mix_norm": 1.0 + nrm(ks[4], (L, D), 0.02),
        "w_in": nrm(ks[5], (L, D, N_IN), D ** -0.5),
        "conv_a_w": nrm(ks[6], (L, CONV_A, D_A), CONV_A ** -0.5),
        "w_out_a": nrm(ks[7], (L, D_A, D), D_A ** -0.5),
        "conv_b_w": nrm(ks[8], (L, CONV_B, D_B), CONV_B ** -0.5),
        "conv_b_bias": nrm(ks[9], (L, D_B), 0.01),
        "ln_b_gain": 1.0 + nrm(ks[10], (L, D_B), 0.02),
        "ln_b_bias": nrm(ks[11], (L, D_B), 0.01),
        "w_out_b": nrm(ks[12], (L, D_B, D), D_B ** -0.5),
        "conv_c_w": nrm(ks[13], (L, CONV_C, 2 * D_C), CONV_C ** -0.5),
        "ig_bias": nrm(ks[14], (L, H), 0.1),
        "fg_bias": jnp.linspace(3.0, 6.0, H, dtype=jnp.float32)[None, :] + nrm(ks[15], (L, H), 0.1),
        "mh_norm": 1.0 + nrm(ks[16], (L, D_C), 0.02),
        "w_out_c": nrm(ks[17], (L, D_C, D), D_C ** -0.5),
        "w_gate": nrm(ks[18], (L, D, N_BRANCH * D), D ** -0.5),
        "b_gate": nrm(ks[19], (L, N_BRANCH * D), 0.01),
        "w_o": nrm(ks[20], (L, D, D), D ** -0.5),
        "ffn2_norm": 1.0 + nrm(ks[21], (L, D), 0.02),
        "ffn2_w1": nrm(ks[22], (L, D, 2 * D_FF), D ** -0.5),
        "ffn2_w2": nrm(ks[23], (L, D_FF, D), D_FF ** -0.5),
        "final_norm": 1.0 + nrm(ks[24], (D,), 0.02),
    }


def reference(x, ffn1_norm, ffn1_w1, ffn1_w2, mix_norm, w_in, conv_a_w, w_out_a, conv_b_w,
              conv_b_bias, ln_b_gain, ln_b_bias, w_out_b, conv_c_w, ig_bias, fg_bias, mh_norm,
              w_out_c, w_gate, b_gate, w_o, ffn2_norm, ffn2_w1, ffn2_w2, final_norm):
    for l in range(DEPTH):
        x = x + 0.5 * _swiglu(_rmsnorm(x, ffn1_norm[l]), ffn1_w1[l], ffn1_w2[l])
        h = _rmsnorm(x, mix_norm[l])
        x = x + _hybrid_mixer(h, w_in[l], conv_a_w[l], w_out_a[l], conv_b_w[l], conv_b_bias[l],
                              ln_b_gain[l], ln_b_bias[l], w_out_b[l], conv_c_w[l], ig_bias[l],
                              fg_bias[l], mh_norm[l], w_out_c[l], w_gate[l], b_gate[l], w_o[l])
        x = x + 0.5 * _swiglu(_rmsnorm(x, ffn2_norm[l]), ffn2_w1[l], ffn2_w2[l])
    return _rmsnorm(x, final_norm)
```

```python
import functools
import math

import jax
import jax.numpy as jnp
from jax import lax
from jax.experimental import pallas as pl
from jax.experimental.pallas import tpu as pltpu

EPS = 1e-6
F32 = jnp.float32
BF16 = jnp.bfloat16

LANES = 128
SUBLANES = 8
VMEM_LIMIT_BYTES = 56 * 1024 * 1024

MLSTM_CHUNK = 256
CONV_ROW_BLOCK = 64


def _pick(n, *cands):
    for c in cands:
        if c <= n and n % c == 0:
            return c
    return n


def _params(n_axes):
    return pltpu.CompilerParams(
        dimension_semantics=("arbitrary",) * n_axes, vmem_limit_bytes=VMEM_LIMIT_BYTES)


def _dot(a, b):
    return jnp.dot(a, b, preferred_element_type=F32)


def _rms(x, g):
    return x * lax.rsqrt(jnp.mean(x * x, axis=-1, keepdims=True) + EPS) * g


def _ffn_body(x_ref, g_ref, w1g_ref, w1u_ref, w2_ref, gn_ref, *refs, emit_x, emit_h):
    outs = list(refs[: int(emit_x) + int(emit_h)])
    h_sc, acc_sc = refs[int(emit_x) + int(emit_h):]
    f = pl.program_id(1)

    @pl.when(f == 0)
    def _():
        h_sc[...] = _rms(x_ref[...], g_ref[...]).astype(BF16)
        acc_sc[...] = jnp.zeros_like(acc_sc)

    h = h_sc[...]
    gate = _dot(h, w1g_ref[...])
    up = _dot(h, w1u_ref[...])
    p = (gate * jax.nn.sigmoid(gate) * up).astype(BF16)
    acc_sc[...] += _dot(p, w2_ref[...])

    @pl.when(f == pl.num_programs(1) - 1)
    def _():
        xn = x_ref[...] + 0.5 * acc_sc[...]
        if emit_x:
            outs[0][...] = xn
        if emit_h:
            outs[-1][...] = _rms(xn, gn_ref[...]).astype(outs[-1].dtype)


def _ffn(x, g, w1g, w1u, w2, gn, *, emit_x, emit_h, h_dtype):
    s, d = x.shape
    fp = w1g.shape[1]
    tm = _pick(s, 512, 256, 128)
    tf = _pick(fp, 512, 256, 128)
    row = lambda i, f: (i, 0)
    out_shape, out_specs = [], []
    if emit_x:
        out_shape.append(jax.ShapeDtypeStruct((s, d), F32))
        out_specs.append(pl.BlockSpec((tm, d), row))
    if emit_h:
        out_shape.append(jax.ShapeDtypeStruct((s, d), h_dtype))
        out_specs.append(pl.BlockSpec((tm, d), row))
    return pl.pallas_call(
        functools.partial(_ffn_body, emit_x=emit_x, emit_h=emit_h),
        grid=(s // tm, fp // tf),
        in_specs=[
            pl.BlockSpec((tm, d), row),
            pl.BlockSpec((1, d), lambda i, f: (0, 0)),
            pl.BlockSpec((d, tf), lambda i, f: (0, f)),
            pl.BlockSpec((d, tf), lambda i, f: (0, f)),
            pl.BlockSpec((tf, d), lambda i, f: (f, 0)),
            pl.BlockSpec((1, d), lambda i, f: (0, 0)),
        ],
        out_specs=out_specs,
        out_shape=out_shape,
        scratch_shapes=[pltpu.VMEM((tm, d), BF16), pltpu.VMEM((tm, d), F32)],
        compiler_params=_params(2),
        name="ffn",
    )(x, g, w1g, w1u, w2, gn)


def _reset_halo(win_sc, halo):
    @pl.when(pl.program_id(1) == 0)
    def _():
        win_sc[0:halo, :] = jnp.zeros((halo, win_sc.shape[1]), F32)


def _carry_halo(win_sc, halo, tm):
    win_sc[0:halo, :] = win_sc[tm:tm + halo, :]


def _conv_taps(win_sc, cw_ref, halo, tm):
    k_width = cw_ref.shape[0]
    acc = None
    for k in range(k_width):
        term = cw_ref[k:k + 1, :] * win_sc[pl.ds(halo - (k_width - 1 - k), tm), :]
        acc = term if acc is None else acc + term
    return acc


def _proj_a_body(h_ref, wb_ref, wc_ref, wx_ref, cw_ref, o_ref, win_sc, *, halo):
    tm = h_ref.shape[0]
    _reset_halo(win_sc, halo)
    h = h_ref[...]
    win_sc[halo:halo + tm, :] = _dot(h, wc_ref[...]) * _dot(h, wx_ref[...])
    conv = _conv_taps(win_sc, cw_ref, halo, tm)
    o_ref[...] = (_dot(h, wb_ref[...]) * conv).astype(o_ref.dtype)
    _carry_halo(win_sc, halo, tm)


def _proj_a(h, w_in, conv_w, d_a):
    s, d = h.shape
    tm = _pick(s, 512, 256, 128)
    tn = _pick(d_a, 512, 256, 128)
    nj = d_a // tn
    halo = SUBLANES
    wspec = lambda off: pl.BlockSpec((d, tn), lambda j, i, off=off: (0, off + j))
    return pl.pallas_call(
        functools.partial(_proj_a_body, halo=halo),
        grid=(nj, s // tm),
        in_specs=[
            pl.BlockSpec((tm, d), lambda j, i: (i, 0)),
            wspec(0), wspec(nj), wspec(2 * nj),
            pl.BlockSpec((conv_w.shape[0], tn), lambda j, i: (0, j)),
        ],
        out_specs=pl.BlockSpec((tm, tn), lambda j, i: (i, j)),
        out_shape=jax.ShapeDtypeStruct((s, d_a), BF16),
        scratch_shapes=[pltpu.VMEM((halo + tm, tn), F32)],
        compiler_params=_params(2),
        name="proj_a",
    )(h, w_in, w_in, w_in, conv_w)


def _proj_b_body(h_ref, wv_ref, wg_ref, cw_ref, cb_ref, lg_ref, lb_ref, o_ref, win_sc, z_sc, *, halo):
    tm = h_ref.shape[0]
    n = o_ref.shape[1]
    k_width = cw_ref.shape[0]
    _reset_halo(win_sc, halo)
    h = h_ref[...]
    win_sc[halo:halo + tm, :] = _dot(h, wv_ref[...]) * jax.nn.sigmoid(_dot(h, wg_ref[...]))
    rb = _pick(tm, CONV_ROW_BLOCK)
    for c0 in range(0, n, LANES):
        for r0 in range(0, tm, rb):
            acc = jnp.zeros((rb, LANES), F32) + cb_ref[:, c0:c0 + LANES]
            for k in range(k_width):
                acc = acc + (cw_ref[k:k + 1, c0:c0 + LANES]
                             * win_sc[r0 + halo - (k_width - 1 - k):r0 + halo - (k_width - 1 - k) + rb,
                                      c0:c0 + LANES])
            z_sc[r0:r0 + rb, c0:c0 + LANES] = acc
    z = z_sc[...]
    zc = z - jnp.mean(z, axis=-1, keepdims=True)
    y = zc * lax.rsqrt(jnp.mean(zc * zc, axis=-1, keepdims=True) + EPS) * lg_ref[...] + lb_ref[...]
    o_ref[...] = (y * jax.nn.sigmoid(y)).astype(o_ref.dtype)
    _carry_halo(win_sc, halo, tm)


def _proj_b(h, w_in, conv_w, conv_b, ln_g, ln_b, col0):
    s, d = h.shape
    d_b = conv_w.shape[1]
    tm = _pick(s, 256, 128)
    halo = -(-(conv_w.shape[0] - 1) // SUBLANES) * SUBLANES
    assert col0 % d_b == 0
    jb = col0 // d_b
    vec = pl.BlockSpec((1, d_b), lambda j, i: (0, 0))
    return pl.pallas_call(
        functools.partial(_proj_b_body, halo=halo),
        grid=(1, s // tm),
        in_specs=[
            pl.BlockSpec((tm, d), lambda j, i: (i, 0)),
            pl.BlockSpec((d, d_b), lambda j, i: (0, jb)),
            pl.BlockSpec((d, d_b), lambda j, i: (0, jb + 1)),
            pl.BlockSpec(conv_w.shape, lambda j, i: (0, 0)),
            vec, vec, vec,
        ],
        out_specs=pl.BlockSpec((tm, d_b), lambda j, i: (i, 0)),
        out_shape=jax.ShapeDtypeStruct((s, d_b), BF16),
        scratch_shapes=[pltpu.VMEM((halo + tm, d_b), F32), pltpu.VMEM((tm, d_b), F32)],
        compiler_params=_params(2),
        name="proj_b",
    )(h, w_in, w_in, conv_w, conv_b, ln_g, ln_b)


def _proj_qk_body(h_ref, w_ref, cw_ref, sc_ref, o_ref, win_sc, *, halo):
    tm = h_ref.shape[0]
    _reset_halo(win_sc, halo)
    win_sc[halo:halo + tm, :] = _dot(h_ref[...], w_ref[...])
    y = _conv_taps(win_sc, cw_ref, halo, tm)
    o_ref[...] = (y * jax.nn.sigmoid(y) * sc_ref[...]).astype(o_ref.dtype)
    _carry_halo(win_sc, halo, tm)


def _proj_qk(h, w_in, conv_w, scale, col0):
    s, d = h.shape
    n = conv_w.shape[1]
    tm = _pick(s, 512, 256, 128)
    tn = _pick(math.gcd(n, col0), 512, 256, 128)
    halo = SUBLANES
    j0 = col0 // tn
    return pl.pallas_call(
        functools.partial(_proj_qk_body, halo=halo),
        grid=(n // tn, s // tm),
        in_specs=[
            pl.BlockSpec((tm, d), lambda j, i: (i, 0)),
            pl.BlockSpec((d, tn), lambda j, i: (0, j0 + j)),
            pl.BlockSpec((conv_w.shape[0], tn), lambda j, i: (0, j)),
            pl.BlockSpec((1, tn), lambda j, i: (0, j)),
        ],
        out_specs=pl.BlockSpec((tm, tn), lambda j, i: (i, j)),
        out_shape=jax.ShapeDtypeStruct((s, n), BF16),
        scratch_shapes=[pltpu.VMEM((halo + tm, tn), F32)],
        compiler_params=_params(2),
        name="proj_qk",
    )(h, w_in, conv_w, scale)


def _proj_plain_body(h_ref, w_ref, o_ref, *, gate):
    y = _dot(h_ref[...], w_ref[...])
    if gate:
        y = jax.nn.sigmoid(y)
    o_ref[...] = y.astype(o_ref.dtype)


def _proj_plain(h, w_in, col0, n, *, gate, name):
    s, d = h.shape
    tm = _pick(s, 512, 256, 128)
    tn = _pick(math.gcd(n, col0), 512, 256, 128)
    j0 = col0 // tn
    return pl.pallas_call(
        functools.partial(_proj_plain_body, gate=gate),
        grid=(n // tn, s // tm),
        in_specs=[
            pl.BlockSpec((tm, d), lambda j, i: (i, 0)),
            pl.BlockSpec((d, tn), lambda j, i: (0, j0 + j)),
        ],
        out_specs=pl.BlockSpec((tm, tn), lambda j, i: (i, j)),
        out_shape=jax.ShapeDtypeStruct((s, n), BF16),
        compiler_params=_params(2),
        name=name,
    )(h, w_in)


def _gates_body(h_ref, w_ref, b_ref, gc_ref, gr_ref, *, n_heads):
    chunk = h_ref.shape[0]
    g = _dot(h_ref[...], w_ref[...]) + b_ref[...]
    lf = jnp.minimum(g, 0.0) - jnp.log1p(jnp.exp(-jnp.abs(g)))
    r = lax.broadcasted_iota(jnp.int32, (chunk, chunk), 0)
    c = lax.broadcasted_iota(jnp.int32, (chunk, chunk), 1)
    tril = (c <= r).astype(F32)
    cum = jnp.dot(tril, lf, precision=lax.Precision.HIGHEST, preferred_element_type=F32)
    lane = lax.broadcasted_iota(jnp.int32, g.shape, 1)
    out = jnp.where(lane < n_heads, g, jnp.where(lane < 2 * n_heads, cum, 0.0))
    gc_ref[...] = out
    gr_ref[...] = out.T[0:gr_ref.shape[0], :]


def _gates(h, w_g, b_g, n_heads, chunk):
    s, d = h.shape
    assert 2 * n_heads <= SUBLANES
    return pl.pallas_call(
        functools.partial(_gates_body, n_heads=n_heads),
        grid=(s // chunk,),
        in_specs=[
            pl.BlockSpec((chunk, d), lambda c: (c, 0)),
            pl.BlockSpec((d, LANES), lambda c: (0, 0)),
            pl.BlockSpec((1, LANES), lambda c: (0, 0)),
        ],
        out_specs=[
            pl.BlockSpec((chunk, LANES), lambda c: (c, 0)),
            pl.BlockSpec((SUBLANES, chunk), lambda c: (0, c)),
        ],
        out_shape=[jax.ShapeDtypeStruct((s, LANES), F32), jax.ShapeDtypeStruct((SUBLANES, s), F32)],
        compiler_params=_params(1),
        name="gates",
    )(h, w_g, b_g)


def _mlstm_body(q_ref, k_ref, v_ref, o_ref, gc_ref, gr_ref, mh_ref, out_ref, c_sc, n_sc, m_sc,
                *, n_heads):
    chunk = q_ref.shape[0]
    dh = q_ref.shape[1] // n_heads

    @pl.when(pl.program_id(0) == 0)
    def _():
        c_sc[...] = jnp.zeros_like(c_sc)
        n_sc[...] = jnp.zeros_like(n_sc)
        m_sc[...] = jnp.zeros_like(m_sc)

    r = lax.broadcasted_iota(jnp.int32, (chunk, chunk), 0)
    c = lax.broadcasted_iota(jnp.int32, (chunk, chunk), 1)
    causal = c <= r
    gc = gc_ref[...]
    gr = gr_ref[...]
    for hd in range(n_heads):
        sl = slice(hd * dh, (hd + 1) * dh)
        q, k, v = q_ref[:, sl], k_ref[:, sl], v_ref[:, sl]
        ig_c, b_c = gc[:, hd:hd + 1], gc[:, n_heads + hd:n_heads + hd + 1]
        ig_r, b_r = gr[hd:hd + 1, :], gr[n_heads + hd:n_heads + hd + 1, :]
        m_prev = m_sc[hd * SUBLANES:hd * SUBLANES + 1, 0:1]
        c_prev = c_sc[hd]
        n_prev = n_sc[hd * SUBLANES:hd * SUBLANES + 1, :]

        inter = b_c + m_prev
        dmat = jnp.where(causal, b_c - b_r + ig_r, -jnp.inf)
        m_t = jnp.maximum(inter, jnp.max(dmat, axis=-1, keepdims=True))
        w_inter = jnp.exp(inter - m_t)
        s = lax.dot_general(q, k, (((1,), (1,)), ((), ())), preferred_element_type=F32)
        s = s * jnp.exp(dmat - m_t)
        num = w_inter * _dot(q, c_prev.astype(BF16)) + _dot(s.astype(BF16), v)
        qn = jnp.sum(q.astype(F32) * n_prev, axis=-1, keepdims=True)
        den = w_inter * qn + jnp.sum(s, axis=-1, keepdims=True)
        ht = num / jnp.maximum(jnp.abs(den), jnp.exp(-m_t))
        ht = _rms(ht, mh_ref[:, sl])
        out_ref[:, sl] = (o_ref[:, sl].astype(F32) * ht).astype(out_ref.dtype)

        m_new = m_t[chunk - 1:chunk, :]
        b_last = b_c[chunk - 1:chunk, :]
        decay = jnp.exp(b_last + m_prev - m_new)
        kw = k.astype(F32) * jnp.exp(b_last - b_c + ig_c - m_new)
        c_sc[hd] = decay * c_prev + lax.dot_general(
            kw.astype(BF16), v, (((0,), (0,)), ((), ())), preferred_element_type=F32)
        n_sc[hd * SUBLANES:(hd + 1) * SUBLANES, :] = jnp.broadcast_to(
            decay * n_prev + jnp.sum(kw, axis=0, keepdims=True), (SUBLANES, dh))
        m_sc[hd * SUBLANES:(hd + 1) * SUBLANES, :] = jnp.broadcast_to(m_new, (SUBLANES, LANES))


def _mlstm(qk, v, o, gc, gr, mh, n_heads, chunk):
    s, d_c = v.shape
    dh = d_c // n_heads
    blk = pl.BlockSpec((chunk, d_c), lambda c: (c, 0))
    return pl.pallas_call(
        functools.partial(_mlstm_body, n_heads=n_heads),
        grid=(s // chunk,),
        in_specs=[
            blk,
            pl.BlockSpec((chunk, d_c), lambda c: (c, 1)),
            blk, blk,
            pl.BlockSpec((chunk, LANES), lambda c: (c, 0)),
            pl.BlockSpec((SUBLANES, chunk), lambda c: (0, c)),
            pl.BlockSpec((1, d_c), lambda c: (0, 0)),
        ],
        out_specs=blk,
        out_shape=jax.ShapeDtypeStruct((s, d_c), BF16),
        scratch_shapes=[
            pltpu.VMEM((n_heads, dh, dh), F32),
            pltpu.VMEM((n_heads * SUBLANES, dh), F32),
            pltpu.VMEM((n_heads * SUBLANES, LANES), F32),
        ],
        compiler_params=_params(1),
        name="mlstm",
    )(qk, qk, v, o, gc, gr, mh)


def _mix_body(pa_ref, pb_ref, pc_ref, h_ref, wa_ref, wb_ref, wc_ref, wga_ref, wgb_ref, wgc_ref,
              ba_ref, bb_ref, bc_ref, o_ref):
    h = h_ref[...]
    acc = jax.nn.sigmoid(_dot(h, wga_ref[...]) + ba_ref[...]) * _dot(pa_ref[...], wa_ref[...])
    acc += jax.nn.sigmoid(_dot(h, wgb_ref[...]) + bb_ref[...]) * _dot(pb_ref[...], wb_ref[...])
    acc += jax.nn.sigmoid(_dot(h, wgc_ref[...]) + bc_ref[...]) * _dot(pc_ref[...], wc_ref[...])
    o_ref[...] = acc.astype(o_ref.dtype)


def _mix(pa, pb, pc, h, wa, wb, wc, wg, bg):
    s, d = h.shape
    tm = _pick(s, 512, 256, 128)
    tn = _pick(d, 512, 256, 128)
    nj = d // tn
    rows = lambda n: pl.BlockSpec((tm, n), lambda j, i: (i, 0))
    cols = lambda kdim, off: pl.BlockSpec((kdim, tn), lambda j, i, off=off: (0, off + j))
    return pl.pallas_call(
        _mix_body,
        grid=(nj, s // tm),
        in_specs=[
            rows(pa.shape[1]), rows(pb.shape[1]), rows(pc.shape[1]), rows(d),
            cols(wa.shape[0], 0), cols(wb.shape[0], 0), cols(wc.shape[0], 0),
            cols(d, 0), cols(d, nj), cols(d, 2 * nj),
            cols(1, 0), cols(1, nj), cols(1, 2 * nj),
        ],
        out_specs=pl.BlockSpec((tm, tn), lambda j, i: (i, j)),
        out_shape=jax.ShapeDtypeStruct((s, d), BF16),
        compiler_params=_params(2),
        name="mix",
    )(pa, pb, pc, h, wa, wb, wc, wg, wg, wg, bg, bg, bg)


def _out_body(x_ref, m_ref, w_ref, o_ref):
    o_ref[...] = x_ref[...] + _dot(m_ref[...], w_ref[...])


def _out(x, mix, w_o):
    s, d = x.shape
    tm = _pick(s, 512, 256, 128)
    row = pl.BlockSpec((tm, d), lambda i: (i, 0))
    return pl.pallas_call(
        _out_body,
        grid=(s // tm,),
        in_specs=[row, row, pl.BlockSpec((d, d), lambda i: (0, 0))],
        out_specs=row,
        out_shape=jax.ShapeDtypeStruct((s, d), F32),
        compiler_params=_params(1),
        name="out_proj",
    )(x, mix, w_o)


def _ffn_weights(w1, w2):
    d_ff = w2.shape[0]
    pad = -d_ff % 512
    w1g = jnp.pad(w1[:, :d_ff].astype(BF16), ((0, 0), (0, pad)))
    w1u = jnp.pad(w1[:, d_ff:].astype(BF16), ((0, 0), (0, pad)))
    return w1g, w1u, jnp.pad(w2.astype(BF16), ((0, pad), (0, 0)))


def kernel(x, ffn1_norm, ffn1_w1, ffn1_w2, mix_norm, w_in, conv_a_w, w_out_a, conv_b_w, conv_b_bias,
           ln_b_gain, ln_b_bias, w_out_b, conv_c_w, ig_bias, fg_bias, mh_norm, w_out_c, w_gate, b_gate,
           w_o, ffn2_norm, ffn2_w1, ffn2_w2, final_norm):
    bsz, s, d = x.shape
    depth = ffn1_norm.shape[0]
    d_a, d_b, d_c = conv_a_w.shape[-1], conv_b_w.shape[-1], mh_norm.shape[-1]
    n_heads = ig_bias.shape[-1]
    chunk = _pick(s, MLSTM_CHUNK)
    col_b = 3 * d_a
    col_q = col_b + 2 * d_b
    col_v = col_q + 2 * d_c
    col_o = col_v + d_c
    col_g = col_o + d_c
    vec = lambda a: a.reshape(1, -1).astype(F32)
    qk_scale = jnp.concatenate(
        [jnp.full((1, d_c), (d_c // n_heads) ** -0.5, F32), jnp.ones((1, d_c), F32)], axis=-1)

    outs = []
    for b in range(bsz):
        xb = x[b]
        for l in range(depth):
            w1g, w1u, w2 = _ffn_weights(ffn1_w1[l], ffn1_w2[l])
            xb, h = _ffn(xb, vec(ffn1_norm[l]), w1g, w1u, w2, vec(mix_norm[l]),
                         emit_x=True, emit_h=True, h_dtype=BF16)

            w_in_b = w_in[l].astype(BF16)
            pa = _proj_a(h, w_in_b, conv_a_w[l], d_a)
            pb = _proj_b(h, w_in_b, conv_b_w[l], vec(conv_b_bias[l]), vec(ln_b_gain[l]),
                         vec(ln_b_bias[l]), col_b)
            qk = _proj_qk(h, w_in_b, conv_c_w[l], qk_scale, col_q)
            v = _proj_plain(h, w_in_b, col_v, d_c, gate=False, name="proj_v")
            o = _proj_plain(h, w_in_b, col_o, d_c, gate=True, name="proj_o")
            w_g = jnp.pad(w_in_b[:, col_g:], ((0, 0), (0, LANES - 2 * n_heads)))
            b_g = jnp.pad(jnp.concatenate([ig_bias[l], fg_bias[l]]).reshape(1, -1).astype(F32),
                          ((0, 0), (0, LANES - 2 * n_heads)))
            gc, gr = _gates(h, w_g, b_g, n_heads, chunk)
            pc = _mlstm(qk, v, o, gc, gr, vec(mh_norm[l]), n_heads, chunk)

            mix = _mix(pa, pb, pc, h, w_out_a[l].astype(BF16), w_out_b[l].astype(BF16),
                       w_out_c[l].astype(BF16), w_gate[l].astype(BF16), vec(b_gate[l]))
            xb = _out(xb, mix, w_o[l].astype(BF16))

            w1g, w1u, w2 = _ffn_weights(ffn2_w1[l], ffn2_w2[l])
            last = l == depth - 1
            res = _ffn(xb, vec(ffn2_norm[l]), w1g, w1u, w2, vec(final_norm),
                       emit_x=not last, emit_h=last, h_dtype=F32)
            xb = res[0]
        outs.append(xb)
    return jnp.stack(outs, axis=0)
```

```python
import functools
import math

import jax
import jax.numpy as jnp
from jax import lax
from jax.experimental import pallas as pl
from jax.experimental.pallas import tpu as pltpu

EPS = 1e-6
F32 = jnp.float32
BF16 = jnp.bfloat16

LANES = 128
SUBLANES = 8
VMEM_LIMIT_BYTES = 56 * 1024 * 1024

MLSTM_CHUNK = 256
EPI_ROW_BLOCK = 64
FFN_TILE_F = 512
MXU_PIECE = 256


def _pick(n, *cands):
    for c in cands:
        if c <= n and n % c == 0:
            return c
    return n


def _params(n_axes):
    return pltpu.CompilerParams(
        dimension_semantics=("arbitrary",) * n_axes, vmem_limit_bytes=VMEM_LIMIT_BYTES)


def _dot(a, b):
    return jnp.dot(a, b, preferred_element_type=F32)


def _rms(x, g):
    return x * lax.rsqrt(jnp.mean(x * x, axis=-1, keepdims=True) + EPS) * g


def _silu(x):
    return x * jax.nn.sigmoid(x)


def _ffn_body(x_ref, g_ref, w1g_ref, w1u_ref, w2_ref, gn_ref, *refs, emit_x, emit_h):
    outs = list(refs[: int(emit_x) + int(emit_h)])
    h_sc, acc_sc = refs[int(emit_x) + int(emit_h):]
    f = pl.program_id(1)

    @pl.when(f == 0)
    def _():
        h_sc[...] = _rms(x_ref[...], g_ref[...]).astype(BF16)
        acc_sc[...] = jnp.zeros_like(acc_sc)

    h = h_sc[...]
    gate = _dot(h, w1g_ref[...])
    up = _dot(h, w1u_ref[...])
    p = (_silu(gate) * up).astype(BF16)
    acc_sc[...] += _dot(p, w2_ref[...])

    @pl.when(f == pl.num_programs(1) - 1)
    def _():
        xn = x_ref[...] + 0.5 * acc_sc[...]
        if emit_x:
            outs[0][...] = xn
        if emit_h:
            outs[-1][...] = _rms(xn, gn_ref[...]).astype(outs[-1].dtype)


def _ffn(x, g, w1, w2, layer, gn, *, emit_x, emit_h, h_dtype):
    s, d = x.shape
    fp = w2.shape[1]
    tm = _pick(s, 512, 256, 128)
    tf = _pick(fp, FFN_TILE_F, 256, 128)
    row = lambda i, f: (i, 0)
    out_shape, out_specs = [], []
    if emit_x:
        out_shape.append(jax.ShapeDtypeStruct((s, d), F32))
        out_specs.append(pl.BlockSpec((tm, d), row))
    if emit_h:
        out_shape.append(jax.ShapeDtypeStruct((s, d), h_dtype))
        out_specs.append(pl.BlockSpec((tm, d), row))
    return pl.pallas_call(
        functools.partial(_ffn_body, emit_x=emit_x, emit_h=emit_h),
        grid=(s // tm, fp // tf),
        in_specs=[
            pl.BlockSpec((tm, d), row),
            pl.BlockSpec((1, d), lambda i, f: (0, 0)),
            pl.BlockSpec((None, None, d, tf), lambda i, f: (layer, 0, 0, f)),
            pl.BlockSpec((None, None, d, tf), lambda i, f: (layer, 1, 0, f)),
            pl.BlockSpec((None, tf, d), lambda i, f: (layer, f, 0)),
            pl.BlockSpec((1, d), lambda i, f: (0, 0)),
        ],
        out_specs=out_specs,
        out_shape=out_shape,
        scratch_shapes=[pltpu.VMEM((tm, d), BF16), pltpu.VMEM((tm, d), F32)],
        compiler_params=_params(2),
        name="ffn",
    )(x, g, w1, w1, w2, gn)


def _proj_a_body(h_ref, wb_ref, wc_ref, wx_ref, cw_ref, o_ref, win_sc, *, halo):
    tm = h_ref.shape[0]
    k_width = cw_ref.shape[0]

    @pl.when(pl.program_id(1) == 0)
    def _():
        win_sc[0:halo, :] = jnp.zeros((halo, win_sc.shape[1]), F32)

    h = h_ref[...]
    win_sc[halo:halo + tm, :] = _dot(h, wc_ref[...]) * _dot(h, wx_ref[...])
    conv = None
    for k in range(k_width):
        term = cw_ref[k:k + 1, :] * win_sc[pl.ds(halo - (k_width - 1 - k), tm), :]
        conv = term if conv is None else conv + term
    o_ref[...] = (_dot(h, wb_ref[...]) * conv).astype(o_ref.dtype)
    win_sc[0:halo, :] = win_sc[tm:tm + halo, :]


def _proj_a(h, w_in, layer, conv_w, d_a):
    s, d = h.shape
    tm = _pick(s, 512, 256, 128)
    tn = _pick(d_a, 512, 256, 128)
    nj = d_a // tn
    halo = SUBLANES
    wspec = lambda off: pl.BlockSpec((None, d, tn), lambda j, i, off=off: (layer, 0, off + j))
    return pl.pallas_call(
        functools.partial(_proj_a_body, halo=halo),
        grid=(nj, s // tm),
        in_specs=[
            pl.BlockSpec((tm, d), lambda j, i: (i, 0)),
            wspec(0), wspec(nj), wspec(2 * nj),
            pl.BlockSpec((None, conv_w.shape[1], tn), lambda j, i: (layer, 0, j)),
        ],
        out_specs=pl.BlockSpec((tm, tn), lambda j, i: (i, j)),
        out_shape=jax.ShapeDtypeStruct((s, d_a), BF16),
        scratch_shapes=[pltpu.VMEM((halo + tm, tn), F32)],
        compiler_params=_params(2),
        name="proj_a",
    )(h, w_in, w_in, w_in, conv_w)


def _halo_reset(win_sc, halo):
    @pl.when(pl.program_id(1) == 0)
    def _():
        win_sc[0:halo, :] = jnp.zeros((halo, win_sc.shape[1]), F32)


def _halo_carry(win_sc, halo, tm):
    win_sc[0:halo, :] = win_sc[tm:tm + halo, :]


def _proj_qk_body(h_ref, w_ref, cw_ref, sc_ref, o_ref, win_sc, *, halo):
    tm, tn = o_ref.shape
    k_width = cw_ref.shape[0]
    rb = _pick(tm, EPI_ROW_BLOCK)
    _halo_reset(win_sc, halo)
    h = h_ref[...]
    pc = _pick(tn, MXU_PIECE, LANES)
    for p0 in range(0, tn, pc):
        ps = slice(p0, p0 + pc)
        win_sc[halo:halo + tm, ps] = _dot(h, w_ref[:, ps])
        for c0 in range(p0, p0 + pc, LANES):
            cs = slice(c0, c0 + LANES)
            for r0 in range(0, tm, rb):
                y = None
                for k in range(k_width):
                    lo = halo + r0 - (k_width - 1 - k)
                    term = cw_ref[k:k + 1, cs] * win_sc[lo:lo + rb, cs]
                    y = term if y is None else y + term
                o_ref[r0:r0 + rb, cs] = (_silu(y) * sc_ref[:, cs]).astype(o_ref.dtype)
    _halo_carry(win_sc, halo, tm)


def _proj_qk(h, w_in, layer, col0, conv_w, scale):
    s, d = h.shape
    k_width, n = conv_w.shape[1:]
    tm = _pick(s, 512, 256, 128)
    tn = _pick(math.gcd(n, col0), 1024, 512, 256, 128)
    halo = SUBLANES
    j0 = col0 // tn
    return pl.pallas_call(
        functools.partial(_proj_qk_body, halo=halo),
        grid=(n // tn, s // tm),
        in_specs=[
            pl.BlockSpec((tm, d), lambda j, i: (i, 0)),
            pl.BlockSpec((None, d, tn), lambda j, i: (layer, 0, j0 + j)),
            pl.BlockSpec((None, k_width, tn), lambda j, i: (layer, 0, j)),
            pl.BlockSpec((1, tn), lambda j, i: (0, j)),
        ],
        out_specs=pl.BlockSpec((tm, tn), lambda j, i: (i, j)),
        out_shape=jax.ShapeDtypeStruct((s, n), BF16),
        scratch_shapes=[pltpu.VMEM((halo + tm, tn), F32)],
        compiler_params=_params(2),
        name="proj_qk",
    )(h, w_in, conv_w, scale)


def _conv_wide(win_sc, cw_ref, cb_ref, z_sc, halo, tm, c_lo, c_hi):
    k_width = cw_ref.shape[0]
    rb = _pick(tm, EPI_ROW_BLOCK)
    for c0 in range(c_lo, c_hi, LANES):
        cs = slice(c0, c0 + LANES)
        for r0 in range(0, tm, rb):
            z = jnp.zeros((rb, LANES), F32) + cb_ref[:, cs]
            for r in range(min(SUBLANES, k_width)):
                y = None
                for j in range(r, k_width, SUBLANES):
                    lo = halo + r0 - SUBLANES - (j - r)
                    term = cw_ref[k_width - 1 - j:k_width - j, cs] * win_sc[lo:lo + rb + SUBLANES, cs]
                    y = term if y is None else y + term
                z = z + y[SUBLANES - r:SUBLANES - r + rb, :]
            z_sc[r0:r0 + rb, cs] = z


def _proj_b_body(h_ref, wv_ref, wg_ref, cw_ref, cb_ref, lg_ref, lb_ref, o_ref, win_sc, z_sc, *, halo):
    tm, n = o_ref.shape
    rb = _pick(tm, EPI_ROW_BLOCK)
    _halo_reset(win_sc, halo)
    h = h_ref[...]
    pc = _pick(n, MXU_PIECE, LANES)
    for p0 in range(0, n, pc):
        ps = slice(p0, p0 + pc)
        win_sc[halo:halo + tm, ps] = _dot(h, wv_ref[:, ps]) * jax.nn.sigmoid(_dot(h, wg_ref[:, ps]))
        _conv_wide(win_sc, cw_ref, cb_ref, z_sc, halo, tm, p0, p0 + pc)
    for r0 in range(0, tm, rb):
        z = z_sc[r0:r0 + rb, :]
        zc = z - jnp.mean(z, axis=-1, keepdims=True)
        y = zc * lax.rsqrt(jnp.mean(zc * zc, axis=-1, keepdims=True) + EPS) * lg_ref[...] + lb_ref[...]
        o_ref[r0:r0 + rb, :] = _silu(y).astype(o_ref.dtype)
    _halo_carry(win_sc, halo, tm)


def _proj_b(h, w_in, layer, conv_w, conv_b, ln_g, ln_b, col0):
    s, d = h.shape
    k_width, d_b = conv_w.shape[1:]
    tm = _pick(s, 256, 128)
    halo = SUBLANES * (1 + (k_width - 1) // SUBLANES)
    assert col0 % d_b == 0
    jb = col0 // d_b
    vec = pl.BlockSpec((1, d_b), lambda j, i: (0, 0))
    return pl.pallas_call(
        functools.partial(_proj_b_body, halo=halo),
        grid=(1, s // tm),
        in_specs=[
            pl.BlockSpec((tm, d), lambda j, i: (i, 0)),
            pl.BlockSpec((None, d, d_b), lambda j, i: (layer, 0, jb)),
            pl.BlockSpec((None, d, d_b), lambda j, i: (layer, 0, jb + 1)),
            pl.BlockSpec((None, k_width, d_b), lambda j, i: (layer, 0, 0)),
            vec, vec, vec,
        ],
        out_specs=pl.BlockSpec((tm, d_b), lambda j, i: (i, 0)),
        out_shape=jax.ShapeDtypeStruct((s, d_b), BF16),
        scratch_shapes=[pltpu.VMEM((halo + tm, d_b), F32), pltpu.VMEM((tm, d_b), F32)],
        compiler_params=_params(2),
        name="proj_b",
    )(h, w_in, w_in, conv_w, conv_b, ln_g, ln_b)


def _proj_plain_body(h_ref, w_ref, o_ref, *, gate):
    h = h_ref[...]
    pc = _pick(o_ref.shape[1], MXU_PIECE, LANES)
    for p0 in range(0, o_ref.shape[1], pc):
        ps = slice(p0, p0 + pc)
        y = _dot(h, w_ref[:, ps])
        if gate:
            y = jax.nn.sigmoid(y)
        o_ref[:, ps] = y.astype(o_ref.dtype)


def _proj_plain(h, w_in, layer, col0, n, *, gate, name):
    s, d = h.shape
    tm = _pick(s, 512, 256, 128)
    tn = _pick(math.gcd(n, col0), 1024, 512, 256, 128)
    j0 = col0 // tn
    return pl.pallas_call(
        functools.partial(_proj_plain_body, gate=gate),
        grid=(n // tn, s // tm),
        in_specs=[
            pl.BlockSpec((tm, d), lambda j, i: (i, 0)),
            pl.BlockSpec((None, d, tn), lambda j, i: (layer, 0, j0 + j)),
        ],
        out_specs=pl.BlockSpec((tm, tn), lambda j, i: (i, j)),
        out_shape=jax.ShapeDtypeStruct((s, n), BF16),
        compiler_params=_params(2),
        name=name,
    )(h, w_in)


def _gates_body(h_ref, w_ref, b_ref, gc_ref, gr_ref, *, n_heads):
    chunk = h_ref.shape[0]
    g = _dot(h_ref[...], w_ref[...]) + b_ref[...]
    lf = jnp.minimum(g, 0.0) - jnp.log1p(jnp.exp(-jnp.abs(g)))
    r = lax.broadcasted_iota(jnp.int32, (chunk, chunk), 0)
    c = lax.broadcasted_iota(jnp.int32, (chunk, chunk), 1)
    tril = (c <= r).astype(F32)
    cum = jnp.dot(tril, lf, precision=lax.Precision.HIGHEST, preferred_element_type=F32)
    lane = lax.broadcasted_iota(jnp.int32, g.shape, 1)
    out = jnp.where(lane < n_heads, g, jnp.where(lane < 2 * n_heads, cum, 0.0))
    gc_ref[...] = out
    gr_ref[...] = out.T[0:gr_ref.shape[0], :]


def _gates(h, w_g, b_g, n_heads, chunk):
    s, d = h.shape
    assert 2 * n_heads <= SUBLANES
    return pl.pallas_call(
        functools.partial(_gates_body, n_heads=n_heads),
        grid=(s // chunk,),
        in_specs=[
            pl.BlockSpec((chunk, d), lambda c: (c, 0)),
            pl.BlockSpec((d, LANES), lambda c: (0, 0)),
            pl.BlockSpec((1, LANES), lambda c: (0, 0)),
        ],
        out_specs=[
            pl.BlockSpec((chunk, LANES), lambda c: (c, 0)),
            pl.BlockSpec((SUBLANES, chunk), lambda c: (0, c)),
        ],
        out_shape=[jax.ShapeDtypeStruct((s, LANES), F32), jax.ShapeDtypeStruct((SUBLANES, s), F32)],
        compiler_params=_params(1),
        name="gates",
    )(h, w_g, b_g)


def _mlstm_body(q_ref, k_ref, v_ref, o_ref, gc_ref, gr_ref, mh_ref, out_ref, c_sc, n_sc, m_sc,
                *, n_heads):
    chunk = q_ref.shape[0]
    dh = q_ref.shape[1] // n_heads

    @pl.when(pl.program_id(0) == 0)
    def _():
        c_sc[...] = jnp.zeros_like(c_sc)
        n_sc[...] = jnp.zeros_like(n_sc)
        m_sc[...] = jnp.zeros_like(m_sc)

    r = lax.broadcasted_iota(jnp.int32, (chunk, chunk), 0)
    c = lax.broadcasted_iota(jnp.int32, (chunk, chunk), 1)
    causal = c <= r
    gc = gc_ref[...]
    gr = gr_ref[...]
    for hd in range(n_heads):
        sl = slice(hd * dh, (hd + 1) * dh)
        q, k, v = q_ref[:, sl], k_ref[:, sl], v_ref[:, sl]
        ig_c, b_c = gc[:, hd:hd + 1], gc[:, n_heads + hd:n_heads + hd + 1]
        ig_r, b_r = gr[hd:hd + 1, :], gr[n_heads + hd:n_heads + hd + 1, :]
        m_prev = m_sc[hd * SUBLANES:hd * SUBLANES + 1, 0:1]
        c_prev = c_sc[hd]
        n_prev = n_sc[hd * SUBLANES:hd * SUBLANES + 1, :]

        inter = b_c + m_prev
        dmat = jnp.where(causal, b_c - b_r + ig_r, -jnp.inf)
        m_t = jnp.maximum(inter, jnp.max(dmat, axis=-1, keepdims=True))
        w_inter = jnp.exp(inter - m_t)
        s = lax.dot_general(q, k, (((1,), (1,)), ((), ())), preferred_element_type=F32)
        s = s * jnp.exp(dmat - m_t)
        num = w_inter * _dot(q, c_prev.astype(BF16)) + _dot(s.astype(BF16), v)
        qn = jnp.sum(q.astype(F32) * n_prev, axis=-1, keepdims=True)
        den = w_inter * qn + jnp.sum(s, axis=-1, keepdims=True)
        ht = num / jnp.maximum(jnp.abs(den), jnp.exp(-m_t))
        ht = _rms(ht, mh_ref[:, sl])
        out_ref[:, sl] = (o_ref[:, sl].astype(F32) * ht).astype(out_ref.dtype)

        m_new = m_t[chunk - 1:chunk, :]
        b_last = b_c[chunk - 1:chunk, :]
        decay = jnp.exp(b_last + m_prev - m_new)
        kw = k.astype(F32) * jnp.exp(b_last - b_c + ig_c - m_new)
        c_sc[hd] = decay * c_prev + lax.dot_general(
            kw.astype(BF16), v, (((0,), (0,)), ((), ())), preferred_element_type=F32)
        n_sc[hd * SUBLANES:(hd + 1) * SUBLANES, :] = jnp.broadcast_to(
            decay * n_prev + jnp.sum(kw, axis=0, keepdims=True), (SUBLANES, dh))
        m_sc[hd * SUBLANES:(hd + 1) * SUBLANES, :] = jnp.broadcast_to(m_new, (SUBLANES, LANES))


def _mlstm(qk, v, o, gc, gr, mh, n_heads, chunk):
    s, d_c = v.shape
    dh = d_c // n_heads
    blk = pl.BlockSpec((chunk, d_c), lambda c: (c, 0))
    return pl.pallas_call(
        functools.partial(_mlstm_body, n_heads=n_heads),
        grid=(s // chunk,),
        in_specs=[
            blk,
            pl.BlockSpec((chunk, d_c), lambda c: (c, 1)),
            blk, blk,
            pl.BlockSpec((chunk, LANES), lambda c: (c, 0)),
            pl.BlockSpec((SUBLANES, chunk), lambda c: (0, c)),
            pl.BlockSpec((1, d_c), lambda c: (0, 0)),
        ],
        out_specs=blk,
        out_shape=jax.ShapeDtypeStruct((s, d_c), BF16),
        scratch_shapes=[
            pltpu.VMEM((n_heads, dh, dh), F32),
            pltpu.VMEM((n_heads * SUBLANES, dh), F32),
            pltpu.VMEM((n_heads * SUBLANES, LANES), F32),
        ],
        compiler_params=_params(1),
        name="mlstm",
    )(qk, qk, v, o, gc, gr, mh)


def _mix_body(pa_ref, pb_ref, pc_ref, h_ref, wa_ref, wb_ref, wc_ref, wga_ref, wgb_ref, wgc_ref,
              ba_ref, bb_ref, bc_ref, o_ref):
    h = h_ref[...]
    acc = jax.nn.sigmoid(_dot(h, wga_ref[...]) + ba_ref[...]) * _dot(pa_ref[...], wa_ref[...])
    acc += jax.nn.sigmoid(_dot(h, wgb_ref[...]) + bb_ref[...]) * _dot(pb_ref[...], wb_ref[...])
    acc += jax.nn.sigmoid(_dot(h, wgc_ref[...]) + bc_ref[...]) * _dot(pc_ref[...], wc_ref[...])
    o_ref[...] = acc.astype(o_ref.dtype)


def _mix(pa, pb, pc, h, wa, wb, wc, wg, bg, layer):
    s, d = h.shape
    tm = _pick(s, 512, 256, 128)
    tn = _pick(d, 512, 256, 128)
    nj = d // tn
    rows = lambda n: pl.BlockSpec((tm, n), lambda j, i: (i, 0))
    cols = lambda kdim, off: pl.BlockSpec((None, kdim, tn), lambda j, i, off=off: (layer, 0, off + j))
    return pl.pallas_call(
        _mix_body,
        grid=(nj, s // tm),
        in_specs=[
            rows(pa.shape[1]), rows(pb.shape[1]), rows(pc.shape[1]), rows(d),
            cols(wa.shape[1], 0), cols(wb.shape[1], 0), cols(wc.shape[1], 0),
            cols(d, 0), cols(d, nj), cols(d, 2 * nj),
            cols(1, 0), cols(1, nj), cols(1, 2 * nj),
        ],
        out_specs=pl.BlockSpec((tm, tn), lambda j, i: (i, j)),
        out_shape=jax.ShapeDtypeStruct((s, d), BF16),
        compiler_params=_params(2),
        name="mix",
    )(pa, pb, pc, h, wa, wb, wc, wg, wg, wg, bg, bg, bg)


def _out_body(x_ref, m_ref, w_ref, o_ref):
    o_ref[...] = x_ref[...] + _dot(m_ref[...], w_ref[...])


def _out(x, mix, w_o, layer):
    s, d = x.shape
    tm = _pick(s, 512, 256, 128)
    row = pl.BlockSpec((tm, d), lambda i: (i, 0))
    return pl.pallas_call(
        _out_body,
        grid=(s // tm,),
        in_specs=[row, row, pl.BlockSpec((None, d, d), lambda i: (layer, 0, 0))],
        out_specs=row,
        out_shape=jax.ShapeDtypeStruct((s, d), F32),
        compiler_params=_params(1),
        name="out_proj",
    )(x, mix, w_o)


def _ffn_weights(w1, w2):
    n_l, d, _ = w1.shape
    d_ff = w2.shape[1]
    pad = -d_ff % FFN_TILE_F
    w1 = jnp.transpose(w1.reshape(n_l, d, 2, d_ff), (0, 2, 1, 3))
    w1 = jnp.pad(w1, ((0, 0), (0, 0), (0, 0), (0, pad))).astype(BF16)
    return w1, jnp.pad(w2, ((0, 0), (0, pad), (0, 0))).astype(BF16)


def kernel(x, ffn1_norm, ffn1_w1, ffn1_w2, mix_norm, w_in, conv_a_w, w_out_a, conv_b_w, conv_b_bias,
           ln_b_gain, ln_b_bias, w_out_b, conv_c_w, ig_bias, fg_bias, mh_norm, w_out_c, w_gate, b_gate,
           w_o, ffn2_norm, ffn2_w1, ffn2_w2, final_norm):
    bsz, s, d = x.shape
    depth = ffn1_norm.shape[0]
    d_a, d_b, d_c = conv_a_w.shape[-1], conv_b_w.shape[-1], mh_norm.shape[-1]
    n_heads = ig_bias.shape[-1]
    chunk = _pick(s, MLSTM_CHUNK)
    col_b = 3 * d_a
    col_q = col_b + 2 * d_b
    col_v = col_q + 2 * d_c
    col_o = col_v + d_c
    col_g = col_o + d_c
    vec = lambda a: a.reshape(1, -1).astype(F32)
    qk_scale = jnp.concatenate(
        [jnp.full((1, d_c), (d_c // n_heads) ** -0.5, F32), jnp.ones((1, d_c), F32)], axis=-1)

    f1_w1, f1_w2 = _ffn_weights(ffn1_w1, ffn1_w2)
    f2_w1, f2_w2 = _ffn_weights(ffn2_w1, ffn2_w2)
    w_in_b = w_in.astype(BF16)
    wa_b, wb_b, wc_b = w_out_a.astype(BF16), w_out_b.astype(BF16), w_out_c.astype(BF16)
    wg_b, wo_b = w_gate.astype(BF16), w_o.astype(BF16)
    bg3 = b_gate.reshape(depth, 1, -1).astype(F32)
    gate_pad = ((0, 0), (0, LANES - 2 * n_heads))

    outs = []
    for b in range(bsz):
        xb = x[b]
        for l in range(depth):
            xb, h = _ffn(xb, vec(ffn1_norm[l]), f1_w1, f1_w2, l, vec(mix_norm[l]),
                         emit_x=True, emit_h=True, h_dtype=BF16)

            pa = _proj_a(h, w_in_b, l, conv_a_w, d_a)
            pb = _proj_b(h, w_in_b, l, conv_b_w, vec(conv_b_bias[l]), vec(ln_b_gain[l]),
                         vec(ln_b_bias[l]), col_b)
            qk = _proj_qk(h, w_in_b, l, col_q, conv_c_w, qk_scale)
            v = _proj_plain(h, w_in_b, l, col_v, d_c, gate=False, name="proj_v")
            o = _proj_plain(h, w_in_b, l, col_o, d_c, gate=True, name="proj_o")
            w_g = jnp.pad(w_in_b[l, :, col_g:], gate_pad)
            b_g = jnp.pad(jnp.concatenate([ig_bias[l], fg_bias[l]]).reshape(1, -1).astype(F32), gate_pad)
            gc, gr = _gates(h, w_g, b_g, n_heads, chunk)
            pc = _mlstm(qk, v, o, gc, gr, vec(mh_norm[l]), n_heads, chunk)

            mix = _mix(pa, pb, pc, h, wa_b, wb_b, wc_b, wg_b, bg3, l)
            xb = _out(xb, mix, wo_b, l)

            last = l == depth - 1
            res = _ffn(xb, vec(ffn2_norm[l]), f2_w1, f2_w2, l, vec(final_norm),
                       emit_x=not last, emit_h=last, h_dtype=F32)
            xb = res[0]
        outs.append(xb)
    return outs[0][None] if bsz == 1 else jnp.stack(outs, axis=0)
```

```python
import functools
import math

import jax
import jax.numpy as jnp
from jax import lax
from jax.experimental import pallas as pl
from jax.experimental.pallas import tpu as pltpu

EPS = 1e-6
F32 = jnp.float32
BF16 = jnp.bfloat16

LANES = 128
SUBLANES = 8
VMEM_LIMIT_BYTES = 56 * 1024 * 1024

MLSTM_CHUNK = 256
EPI_ROW_BLOCK = 64
FFN_TILE_F = 512
MXU_PIECE = 256


def _pick(n, *cands):
    for c in cands:
        if c <= n and n % c == 0:
            return c
    return n


def _params(n_axes):
    return pltpu.CompilerParams(
        dimension_semantics=("arbitrary",) * n_axes, vmem_limit_bytes=VMEM_LIMIT_BYTES)


def _dot(a, b):
    return jnp.dot(a, b, preferred_element_type=F32)


def _rms(x, g):
    return x * lax.rsqrt(jnp.mean(x * x, axis=-1, keepdims=True) + EPS) * g


def _silu(x):
    return x * jax.nn.sigmoid(x)


def _ffn_body(x_ref, g_ref, w1g_ref, w1u_ref, w2_ref, gn_ref, *refs, emit_x, emit_h):
    outs = list(refs[: int(emit_x) + int(emit_h)])
    h_sc, acc_sc = refs[int(emit_x) + int(emit_h):]
    f = pl.program_id(1)

    @pl.when(f == 0)
    def _():
        h_sc[...] = _rms(x_ref[...], g_ref[...]).astype(BF16)
        acc_sc[...] = jnp.zeros_like(acc_sc)

    h = h_sc[...]
    gate = _dot(h, w1g_ref[...])
    up = _dot(h, w1u_ref[...])
    p = (_silu(gate) * up).astype(BF16)
    acc_sc[...] += _dot(p, w2_ref[...])

    @pl.when(f == pl.num_programs(1) - 1)
    def _():
        xn = x_ref[...] + 0.5 * acc_sc[...]
        if emit_x:
            outs[0][...] = xn
        if emit_h:
            outs[-1][...] = _rms(xn, gn_ref[...]).astype(outs[-1].dtype)


def _ffn(x, g, w1, w2, layer, gn, *, emit_x, emit_h, h_dtype):
    s, d = x.shape
    fp = w2.shape[1]
    tm = _pick(s, 512, 256, 128)
    tf = _pick(fp, FFN_TILE_F, 256, 128)
    row = lambda i, f: (i, 0)
    out_shape, out_specs = [], []
    if emit_x:
        out_shape.append(jax.ShapeDtypeStruct((s, d), F32))
        out_specs.append(pl.BlockSpec((tm, d), row))
    if emit_h:
        out_shape.append(jax.ShapeDtypeStruct((s, d), h_dtype))
        out_specs.append(pl.BlockSpec((tm, d), row))
    return pl.pallas_call(
        functools.partial(_ffn_body, emit_x=emit_x, emit_h=emit_h),
        grid=(s // tm, fp // tf),
        in_specs=[
            pl.BlockSpec((tm, d), row),
            pl.BlockSpec((1, d), lambda i, f: (0, 0)),
            pl.BlockSpec((None, None, d, tf), lambda i, f: (layer, 0, 0, f)),
            pl.BlockSpec((None, None, d, tf), lambda i, f: (layer, 1, 0, f)),
            pl.BlockSpec((None, tf, d), lambda i, f: (layer, f, 0)),
            pl.BlockSpec((1, d), lambda i, f: (0, 0)),
        ],
        out_specs=out_specs,
        out_shape=out_shape,
        scratch_shapes=[pltpu.VMEM((tm, d), BF16), pltpu.VMEM((tm, d), F32)],
        compiler_params=_params(2),
        name="ffn",
    )(x, g, w1, w1, w2, gn)


def _proj_a_body(h_ref, wb_ref, wc_ref, wx_ref, cw_ref, o_ref, win_sc, *, halo):
    tm = h_ref.shape[0]
    k_width = cw_ref.shape[0]

    @pl.when(pl.program_id(1) == 0)
    def _():
        win_sc[0:halo, :] = jnp.zeros((halo, win_sc.shape[1]), F32)

    h = h_ref[...]
    win_sc[halo:halo + tm, :] = _dot(h, wc_ref[...]) * _dot(h, wx_ref[...])
    conv = None
    for k in range(k_width):
        term = cw_ref[k:k + 1, :] * win_sc[pl.ds(halo - (k_width - 1 - k), tm), :]
        conv = term if conv is None else conv + term
    o_ref[...] = (_dot(h, wb_ref[...]) * conv).astype(o_ref.dtype)
    win_sc[0:halo, :] = win_sc[tm:tm + halo, :]


def _proj_a(h, w_in, layer, conv_w, d_a):
    s, d = h.shape
    tm = _pick(s, 512, 256, 128)
    tn = _pick(d_a, 512, 256, 128)
    nj = d_a // tn
    halo = SUBLANES
    wspec = lambda off: pl.BlockSpec((None, d, tn), lambda j, i, off=off: (layer, 0, off + j))
    return pl.pallas_call(
        functools.partial(_proj_a_body, halo=halo),
        grid=(nj, s // tm),
        in_specs=[
            pl.BlockSpec((tm, d), lambda j, i: (i, 0)),
            wspec(0), wspec(nj), wspec(2 * nj),
            pl.BlockSpec((None, conv_w.shape[1], tn), lambda j, i: (layer, 0, j)),
        ],
        out_specs=pl.BlockSpec((tm, tn), lambda j, i: (i, j)),
        out_shape=jax.ShapeDtypeStruct((s, d_a), BF16),
        scratch_shapes=[pltpu.VMEM((halo + tm, tn), F32)],
        compiler_params=_params(2),
        name="proj_a",
    )(h, w_in, w_in, w_in, conv_w)


def _halo_reset(win_sc, halo):
    @pl.when(pl.program_id(1) == 0)
    def _():
        win_sc[0:halo, :] = jnp.zeros((halo, win_sc.shape[1]), F32)


def _halo_carry(win_sc, halo, tm):
    win_sc[0:halo, :] = win_sc[tm:tm + halo, :]


def _proj_qk_body(h_ref, w_ref, cw_ref, sc_ref, o_ref, win_sc, *, halo):
    tm, tn = o_ref.shape
    k_width = cw_ref.shape[0]
    rb = _pick(tm, EPI_ROW_BLOCK)
    _halo_reset(win_sc, halo)
    h = h_ref[...]
    pc = _pick(tn, MXU_PIECE, LANES)
    for p0 in range(0, tn, pc):
        ps = slice(p0, p0 + pc)
        win_sc[halo:halo + tm, ps] = _dot(h, w_ref[:, ps])
        for c0 in range(p0, p0 + pc, LANES):
            cs = slice(c0, c0 + LANES)
            for r0 in range(0, tm, rb):
                y = None
                for k in range(k_width):
                    lo = halo + r0 - (k_width - 1 - k)
                    term = cw_ref[k:k + 1, cs] * win_sc[lo:lo + rb, cs]
                    y = term if y is None else y + term
                o_ref[r0:r0 + rb, cs] = (_silu(y) * sc_ref[:, cs]).astype(o_ref.dtype)
    _halo_carry(win_sc, halo, tm)


def _proj_qk(h, w_in, layer, col0, conv_w, scale):
    s, d = h.shape
    k_width, n = conv_w.shape[1:]
    tm = _pick(s, 512, 256, 128)
    tn = _pick(math.gcd(n, col0), 1024, 512, 256, 128)
    halo = SUBLANES
    j0 = col0 // tn
    return pl.pallas_call(
        functools.partial(_proj_qk_body, halo=halo),
        grid=(n // tn, s // tm),
        in_specs=[
            pl.BlockSpec((tm, d), lambda j, i: (i, 0)),
            pl.BlockSpec((None, d, tn), lambda j, i: (layer, 0, j0 + j)),
            pl.BlockSpec((None, k_width, tn), lambda j, i: (layer, 0, j)),
            pl.BlockSpec((1, tn), lambda j, i: (0, j)),
        ],
        out_specs=pl.BlockSpec((tm, tn), lambda j, i: (i, j)),
        out_shape=jax.ShapeDtypeStruct((s, n), BF16),
        scratch_shapes=[pltpu.VMEM((halo + tm, tn), F32)],
        compiler_params=_params(2),
        name="proj_qk",
    )(h, w_in, conv_w, scale)


def _conv_wide(win_sc, cw_ref, cb_ref, z_sc, halo, tm, c_lo, c_hi):
    k_width = cw_ref.shape[0]
    rb = _pick(tm, EPI_ROW_BLOCK)
    for c0 in range(c_lo, c_hi, LANES):
        cs = slice(c0, c0 + LANES)
        for r0 in range(0, tm, rb):
            z = jnp.zeros((rb, LANES), F32) + cb_ref[:, cs]
            for r in range(min(SUBLANES, k_width)):
                y = None
                for j in range(r, k_width, SUBLANES):
                    lo = halo + r0 - SUBLANES - (j - r)
                    term = cw_ref[k_width - 1 - j:k_width - j, cs] * win_sc[lo:lo + rb + SUBLANES, cs]
                    y = term if y is None else y + term
                z = z + y[SUBLANES - r:SUBLANES - r + rb, :]
            z_sc[r0:r0 + rb, cs] = z


def _proj_b_body(h_ref, wv_ref, wg_ref, cw_ref, cb_ref, lg_ref, lb_ref, o_ref, win_sc, z_sc, *, halo):
    tm, n = o_ref.shape
    rb = _pick(tm, EPI_ROW_BLOCK)
    _halo_reset(win_sc, halo)
    h = h_ref[...]
    pc = _pick(n, MXU_PIECE, LANES)
    for p0 in range(0, n, pc):
        ps = slice(p0, p0 + pc)
        win_sc[halo:halo + tm, ps] = _dot(h, wv_ref[:, ps]) * jax.nn.sigmoid(_dot(h, wg_ref[:, ps]))
        _conv_wide(win_sc, cw_ref, cb_ref, z_sc, halo, tm, p0, p0 + pc)
    for r0 in range(0, tm, rb):
        z = z_sc[r0:r0 + rb, :]
        zc = z - jnp.mean(z, axis=-1, keepdims=True)
        y = zc * lax.rsqrt(jnp.mean(zc * zc, axis=-1, keepdims=True) + EPS) * lg_ref[...] + lb_ref[...]
        o_ref[r0:r0 + rb, :] = _silu(y).astype(o_ref.dtype)
    _halo_carry(win_sc, halo, tm)


def _proj_b(h, w_in, layer, conv_w, conv_b, ln_g, ln_b, col0):
    s, d = h.shape
    k_width, d_b = conv_w.shape[1:]
    tm = _pick(s, 256, 128)
    halo = SUBLANES * (1 + (k_width - 1) // SUBLANES)
    assert col0 % d_b == 0
    jb = col0 // d_b
    vec = pl.BlockSpec((1, d_b), lambda j, i: (0, 0))
    return pl.pallas_call(
        functools.partial(_proj_b_body, halo=halo),
        grid=(1, s // tm),
        in_specs=[
            pl.BlockSpec((tm, d), lambda j, i: (i, 0)),
            pl.BlockSpec((None, d, d_b), lambda j, i: (layer, 0, jb)),
            pl.BlockSpec((None, d, d_b), lambda j, i: (layer, 0, jb + 1)),
            pl.BlockSpec((None, k_width, d_b), lambda j, i: (layer, 0, 0)),
            vec, vec, vec,
        ],
        out_specs=pl.BlockSpec((tm, d_b), lambda j, i: (i, 0)),
        out_shape=jax.ShapeDtypeStruct((s, d_b), BF16),
        scratch_shapes=[pltpu.VMEM((halo + tm, d_b), F32), pltpu.VMEM((tm, d_b), F32)],
        compiler_params=_params(2),
        name="proj_b",
    )(h, w_in, w_in, conv_w, conv_b, ln_g, ln_b)


def _proj_plain_body(h_ref, w_ref, o_ref, *, gate):
    h = h_ref[...]
    pc = _pick(o_ref.shape[1], MXU_PIECE, LANES)
    for p0 in range(0, o_ref.shape[1], pc):
        ps = slice(p0, p0 + pc)
        y = _dot(h, w_ref[:, ps])
        if gate:
            y = jax.nn.sigmoid(y)
        o_ref[:, ps] = y.astype(o_ref.dtype)


def _proj_plain(h, w_in, layer, col0, n, *, gate, name):
    s, d = h.shape
    tm = _pick(s, 512, 256, 128)
    tn = _pick(math.gcd(n, col0), 1024, 512, 256, 128)
    j0 = col0 // tn
    return pl.pallas_call(
        functools.partial(_proj_plain_body, gate=gate),
        grid=(n // tn, s // tm),
        in_specs=[
            pl.BlockSpec((tm, d), lambda j, i: (i, 0)),
            pl.BlockSpec((None, d, tn), lambda j, i: (layer, 0, j0 + j)),
        ],
        out_specs=pl.BlockSpec((tm, tn), lambda j, i: (i, j)),
        out_shape=jax.ShapeDtypeStruct((s, n), BF16),
        compiler_params=_params(2),
        name=name,
    )(h, w_in)


def _gates_body(h_ref, w_ref, b_ref, gc_ref, gr_ref, *, n_heads, chunk):
    r = lax.broadcasted_iota(jnp.int32, (chunk, chunk), 0)
    c = lax.broadcasted_iota(jnp.int32, (chunk, chunk), 1)
    tril = (c <= r).astype(F32)
    lane = lax.broadcasted_iota(jnp.int32, (chunk, LANES), 1)
    for t0 in range(0, h_ref.shape[0], chunk):
        g = _dot(h_ref[t0:t0 + chunk, :], w_ref[...]) + b_ref[...]
        lf = jnp.minimum(g, 0.0) - jnp.log1p(jnp.exp(-jnp.abs(g)))
        cum = jnp.dot(tril, lf, precision=lax.Precision.HIGHEST, preferred_element_type=F32)
        out = jnp.where(lane < n_heads, g, jnp.where(lane < 2 * n_heads, cum, 0.0))
        gc_ref[t0:t0 + chunk, :] = out
        gr_ref[:, t0:t0 + chunk] = out.T[0:gr_ref.shape[0], :]


def _gates(h, w_g, b_g, n_heads, chunk):
    s, d = h.shape
    assert 2 * n_heads <= SUBLANES
    tm = _pick(s, 4 * chunk, 2 * chunk, chunk)
    return pl.pallas_call(
        functools.partial(_gates_body, n_heads=n_heads, chunk=chunk),
        grid=(s // tm,),
        in_specs=[
            pl.BlockSpec((tm, d), lambda c: (c, 0)),
            pl.BlockSpec((d, LANES), lambda c: (0, 0)),
            pl.BlockSpec((1, LANES), lambda c: (0, 0)),
        ],
        out_specs=[
            pl.BlockSpec((tm, LANES), lambda c: (c, 0)),
            pl.BlockSpec((SUBLANES, tm), lambda c: (0, c)),
        ],
        out_shape=[jax.ShapeDtypeStruct((s, LANES), F32), jax.ShapeDtypeStruct((SUBLANES, s), F32)],
        compiler_params=_params(1),
        name="gates",
    )(h, w_g, b_g)


def _mlstm_body(q_ref, k_ref, v_ref, o_ref, gc_ref, gr_ref, mh_ref, out_ref, c_sc, n_sc, m_sc,
                *, n_heads):
    chunk = q_ref.shape[0]
    dh = q_ref.shape[1] // n_heads

    @pl.when(pl.program_id(0) == 0)
    def _():
        c_sc[...] = jnp.zeros_like(c_sc)
        n_sc[...] = jnp.zeros_like(n_sc)
        m_sc[...] = jnp.zeros_like(m_sc)

    r = lax.broadcasted_iota(jnp.int32, (chunk, chunk), 0)
    c = lax.broadcasted_iota(jnp.int32, (chunk, chunk), 1)
    causal = c <= r
    gc = gc_ref[...]
    gr = gr_ref[...]
    for hd in range(n_heads):
        sl = slice(hd * dh, (hd + 1) * dh)
        q, k, v = q_ref[:, sl], k_ref[:, sl], v_ref[:, sl]
        ig_c, b_c = gc[:, hd:hd + 1], gc[:, n_heads + hd:n_heads + hd + 1]
        ig_r, b_r = gr[hd:hd + 1, :], gr[n_heads + hd:n_heads + hd + 1, :]
        m_prev = m_sc[hd * SUBLANES:hd * SUBLANES + 1, 0:1]
        c_prev = c_sc[hd]
        n_prev = n_sc[hd * SUBLANES:hd * SUBLANES + 1, :]

        inter = b_c + m_prev
        dmat = jnp.where(causal, b_c - b_r + ig_r, -jnp.inf)
        m_t = jnp.maximum(inter, jnp.max(dmat, axis=-1, keepdims=True))
        w_inter = jnp.exp(inter - m_t)
        s = lax.dot_general(q, k, (((1,), (1,)), ((), ())), preferred_element_type=F32)
        s = s * jnp.exp(dmat - m_t)
        num = w_inter * _dot(q, c_prev.astype(BF16)) + _dot(s.astype(BF16), v)
        qn = jnp.sum(q.astype(F32) * n_prev, axis=-1, keepdims=True)
        den = w_inter * qn + jnp.sum(s, axis=-1, keepdims=True)
        ht = num / jnp.maximum(jnp.abs(den), jnp.exp(-m_t))
        ht = _rms(ht, mh_ref[:, sl])
        out_ref[:, sl] = (o_ref[:, sl].astype(F32) * ht).astype(out_ref.dtype)

        m_new = m_t[chunk - 1:chunk, :]
        b_last = b_c[chunk - 1:chunk, :]
        decay = jnp.exp(b_last + m_prev - m_new)
        kw = k.astype(F32) * jnp.exp(b_last - b_c + ig_c - m_new)
        c_sc[hd] = decay * c_prev + lax.dot_general(
            kw.astype(BF16), v, (((0,), (0,)), ((), ())), preferred_element_type=F32)
        n_sc[hd * SUBLANES:(hd + 1) * SUBLANES, :] = jnp.broadcast_to(
            decay * n_prev + jnp.sum(kw, axis=0, keepdims=True), (SUBLANES, dh))
        m_sc[hd * SUBLANES:(hd + 1) * SUBLANES, :] = jnp.broadcast_to(m_new, (SUBLANES, LANES))


def _mlstm(qk, v, o, gc, gr, mh, n_heads, chunk):
    s, d_c = v.shape
    dh = d_c // n_heads
    blk = pl.BlockSpec((chunk, d_c), lambda c: (c, 0))
    return pl.pallas_call(
        functools.partial(_mlstm_body, n_heads=n_heads),
        grid=(s // chunk,),
        in_specs=[
            blk,
            pl.BlockSpec((chunk, d_c), lambda c: (c, 1)),
            blk, blk,
            pl.BlockSpec((chunk, LANES), lambda c: (c, 0)),
            pl.BlockSpec((SUBLANES, chunk), lambda c: (0, c)),
            pl.BlockSpec((1, d_c), lambda c: (0, 0)),
        ],
        out_specs=blk,
        out_shape=jax.ShapeDtypeStruct((s, d_c), BF16),
        scratch_shapes=[
            pltpu.VMEM((n_heads, dh, dh), F32),
            pltpu.VMEM((n_heads * SUBLANES, dh), F32),
            pltpu.VMEM((n_heads * SUBLANES, LANES), F32),
        ],
        compiler_params=_params(1),
        name="mlstm",
    )(qk, qk, v, o, gc, gr, mh)


def _mix_body(pa_ref, pb_ref, pc_ref, h_ref, wa_ref, wb_ref, wc_ref, wga_ref, wgb_ref, wgc_ref,
              ba_ref, bb_ref, bc_ref, o_ref):
    h = h_ref[...]
    acc = jax.nn.sigmoid(_dot(h, wga_ref[...]) + ba_ref[...]) * _dot(pa_ref[...], wa_ref[...])
    acc += jax.nn.sigmoid(_dot(h, wgb_ref[...]) + bb_ref[...]) * _dot(pb_ref[...], wb_ref[...])
    acc += jax.nn.sigmoid(_dot(h, wgc_ref[...]) + bc_ref[...]) * _dot(pc_ref[...], wc_ref[...])
    o_ref[...] = acc.astype(o_ref.dtype)


def _mix(pa, pb, pc, h, wa, wb, wc, wg, bg, layer):
    s, d = h.shape
    tm = _pick(s, 512, 256, 128)
    tn = _pick(d, 512, 256, 128)
    nj = d // tn
    rows = lambda n: pl.BlockSpec((tm, n), lambda j, i: (i, 0))
    cols = lambda kdim, off: pl.BlockSpec((None, kdim, tn), lambda j, i, off=off: (layer, 0, off + j))
    return pl.pallas_call(
        _mix_body,
        grid=(nj, s // tm),
        in_specs=[
            rows(pa.shape[1]), rows(pb.shape[1]), rows(pc.shape[1]), rows(d),
            cols(wa.shape[1], 0), cols(wb.shape[1], 0), cols(wc.shape[1], 0),
            cols(d, 0), cols(d, nj), cols(d, 2 * nj),
            cols(1, 0), cols(1, nj), cols(1, 2 * nj),
        ],
        out_specs=pl.BlockSpec((tm, tn), lambda j, i: (i, j)),
        out_shape=jax.ShapeDtypeStruct((s, d), BF16),
        compiler_params=_params(2),
        name="mix",
    )(pa, pb, pc, h, wa, wb, wc, wg, wg, wg, bg, bg, bg)


def _out_body(x_ref, m_ref, w_ref, o_ref):
    o_ref[...] = x_ref[...] + _dot(m_ref[...], w_ref[...])


def _out(x, mix, w_o, layer):
    s, d = x.shape
    tm = _pick(s, 512, 256, 128)
    row = pl.BlockSpec((tm, d), lambda i: (i, 0))
    return pl.pallas_call(
        _out_body,
        grid=(s // tm,),
        in_specs=[row, row, pl.BlockSpec((None, d, d), lambda i: (layer, 0, 0))],
        out_specs=row,
        out_shape=jax.ShapeDtypeStruct((s, d), F32),
        compiler_params=_params(1),
        name="out_proj",
    )(x, mix, w_o)


def _cast_w1_body(*refs, n_valid):
    ins, o_ref = refs[:-1], refs[-1]
    c = pl.program_id(2)
    for q, x_ref in enumerate(ins):
        valid = c * len(ins) + q < n_valid
        o_ref[:, q * LANES:(q + 1) * LANES] = jnp.where(valid, x_ref[...], 0.0).astype(o_ref.dtype)


def _cast_w2_body(*refs, n_valid):
    ins, o_ref = refs[:-1], refs[-1]
    c = pl.program_id(1)
    for q, x_ref in enumerate(ins):
        valid = c * len(ins) + q < n_valid
        o_ref[q * LANES:(q + 1) * LANES, :] = jnp.where(valid, x_ref[...], 0.0).astype(o_ref.dtype)


def _ffn_weights(w1, w2):
    n_l, d, _ = w1.shape
    d_ff = w2.shape[1]
    assert d_ff % LANES == 0
    nb = d_ff // LANES
    per = FFN_TILE_F // LANES
    fp = -(-d_ff // FFN_TILE_F) * FFN_TILE_F
    w1_specs = [pl.BlockSpec((None, d, LANES),
                             lambda l, g, c, q=q: (l, 0, g * nb + jnp.minimum(c * per + q, nb - 1)))
                for q in range(per)]
    w1b = pl.pallas_call(
        functools.partial(_cast_w1_body, n_valid=nb),
        grid=(n_l, 2, fp // FFN_TILE_F),
        in_specs=w1_specs,
        out_specs=pl.BlockSpec((None, None, d, FFN_TILE_F), lambda l, g, c: (l, g, 0, c)),
        out_shape=jax.ShapeDtypeStruct((n_l, 2, d, fp), BF16),
        compiler_params=_params(3),
        name="cast_w1",
    )(*([w1] * per))
    w2_specs = [pl.BlockSpec((None, LANES, d), lambda l, c, q=q: (l, jnp.minimum(c * per + q, nb - 1), 0))
                for q in range(per)]
    w2b = pl.pallas_call(
        functools.partial(_cast_w2_body, n_valid=nb),
        grid=(n_l, fp // FFN_TILE_F),
        in_specs=w2_specs,
        out_specs=pl.BlockSpec((None, FFN_TILE_F, d), lambda l, c: (l, c, 0)),
        out_shape=jax.ShapeDtypeStruct((n_l, fp, d), BF16),
        compiler_params=_params(2),
        name="cast_w2",
    )(*([w2] * per))
    return w1b, w2b


def kernel(x, ffn1_norm, ffn1_w1, ffn1_w2, mix_norm, w_in, conv_a_w, w_out_a, conv_b_w, conv_b_bias,
           ln_b_gain, ln_b_bias, w_out_b, conv_c_w, ig_bias, fg_bias, mh_norm, w_out_c, w_gate, b_gate,
           w_o, ffn2_norm, ffn2_w1, ffn2_w2, final_norm):
    bsz, s, d = x.shape
    depth = ffn1_norm.shape[0]
    d_a, d_b, d_c = conv_a_w.shape[-1], conv_b_w.shape[-1], mh_norm.shape[-1]
    n_heads = ig_bias.shape[-1]
    chunk = _pick(s, MLSTM_CHUNK)
    col_b = 3 * d_a
    col_q = col_b + 2 * d_b
    col_v = col_q + 2 * d_c
    col_o = col_v + d_c
    col_g = col_o + d_c
    vec = lambda a: a.reshape(1, -1).astype(F32)
    qk_scale = jnp.concatenate(
        [jnp.full((1, d_c), (d_c // n_heads) ** -0.5, F32), jnp.ones((1, d_c), F32)], axis=-1)

    f1_w1, f1_w2 = _ffn_weights(ffn1_w1, ffn1_w2)
    f2_w1, f2_w2 = _ffn_weights(ffn2_w1, ffn2_w2)
    w_in_b = w_in.astype(BF16)
    wa_b, wb_b, wc_b = w_out_a.astype(BF16), w_out_b.astype(BF16), w_out_c.astype(BF16)
    wg_b, wo_b = w_gate.astype(BF16), w_o.astype(BF16)
    bg3 = b_gate.reshape(depth, 1, -1).astype(F32)
    gate_pad = ((0, 0), (0, LANES - 2 * n_heads))

    outs = []
    for b in range(bsz):
        xb = x[b]
        for l in range(depth):
            xb, h = _ffn(xb, vec(ffn1_norm[l]), f1_w1, f1_w2, l, vec(mix_norm[l]),
                         emit_x=True, emit_h=True, h_dtype=BF16)

            pa = _proj_a(h, w_in_b, l, conv_a_w, d_a)
            pb = _proj_b(h, w_in_b, l, conv_b_w, vec(conv_b_bias[l]), vec(ln_b_gain[l]),
                         vec(ln_b_bias[l]), col_b)
            qk = _proj_qk(h, w_in_b, l, col_q, conv_c_w, qk_scale)
            v = _proj_plain(h, w_in_b, l, col_v, d_c, gate=False, name="proj_v")
            o = _proj_plain(h, w_in_b, l, col_o, d_c, gate=True, name="proj_o")
            w_g = jnp.pad(w_in_b[l, :, col_g:], gate_pad)
            b_g = jnp.pad(jnp.concatenate([ig_bias[l], fg_bias[l]]).reshape(1, -1).astype(F32), gate_pad)
            gc, gr = _gates(h, w_g, b_g, n_heads, chunk)
            pc = _mlstm(qk, v, o, gc, gr, vec(mh_norm[l]), n_heads, chunk)

            mix = _mix(pa, pb, pc, h, wa_b, wb_b, wc_b, wg_b, bg3, l)
            xb = _out(xb, mix, wo_b, l)

            last = l == depth - 1
            res = _ffn(xb, vec(ffn2_norm[l]), f2_w1, f2_w2, l, vec(final_norm),
                       emit_x=not last, emit_h=last, h_dtype=F32)
            xb = res[0]
        outs.append(xb)
    return outs[0][None] if bsz == 1 else jnp.stack(outs, axis=0)
```

```python
import functools
import math

import jax
import jax.numpy as jnp
from jax import lax
from jax.experimental import pallas as pl
from jax.experimental.pallas import tpu as pltpu

EPS = 1e-6
F32 = jnp.float32
BF16 = jnp.bfloat16

LANES = 128
SUBLANES = 8
VMEM_LIMIT_BYTES = 56 * 1024 * 1024

MLSTM_CHUNK = 256
EPI_ROW_BLOCK = 64
FFN_TILE_F = 512
MXU_PIECE = 256


def _pick(n, *cands):
    for c in cands:
        if c <= n and n % c == 0:
            return c
    return n


def _params(n_axes):
    return pltpu.CompilerParams(
        dimension_semantics=("arbitrary",) * n_axes, vmem_limit_bytes=VMEM_LIMIT_BYTES)


def _dot(a, b):
    return jnp.dot(a, b, preferred_element_type=F32)


def _rms(x, g):
    return x * lax.rsqrt(jnp.mean(x * x, axis=-1, keepdims=True) + EPS) * g


def _silu(x):
    return x * jax.nn.sigmoid(x)


def _cast_slab(x_ref, o_ref, valid):
    o_ref[...] = jnp.where(valid, x_ref[...], 0.0).astype(o_ref.dtype)


def _ffn_body(x_ref, g_ref, w1g_ref, w1u_ref, w2_ref, gn_ref, *refs, emit_x, emit_h, side):
    n_side_in = 2 if side else 0
    side_in, refs = refs[:n_side_in], refs[n_side_in:]
    n_out = int(emit_x) + int(emit_h)
    outs, side_out = list(refs[:n_out]), refs[n_out:n_out + n_side_in]
    h_sc, acc_sc = refs[n_out + n_side_in:]
    f = pl.program_id(1)

    @pl.when(f == 0)
    def _():
        h_sc[...] = _rms(x_ref[...], g_ref[...]).astype(BF16)
        acc_sc[...] = jnp.zeros_like(acc_sc)

    h = h_sc[...]
    gate = _dot(h, w1g_ref[...])
    up = _dot(h, w1u_ref[...])
    p = (_silu(gate) * up).astype(BF16)
    acc_sc[...] += _dot(p, w2_ref[...])

    @pl.when(f == pl.num_programs(1) - 1)
    def _():
        xn = x_ref[...] + 0.5 * acc_sc[...]
        if emit_x:
            outs[0][...] = xn
        if emit_h:
            outs[-1][...] = _rms(xn, gn_ref[...]).astype(outs[-1].dtype)

    if side:
        n_slab, n_valid = side
        t = pl.program_id(0) * pl.num_programs(1) + f

        @pl.when(t < 2 * n_slab)
        def _():
            _cast_slab(side_in[0], side_out[0], t % n_slab < n_valid)

        @pl.when((t >= 2 * n_slab) & (t < 3 * n_slab))
        def _():
            _cast_slab(side_in[1], side_out[1], t - 2 * n_slab < n_valid)


def _ffn_tiles(s, fp):
    tm = _pick(s, 512, 256, 128)
    tf = _pick(fp, FFN_TILE_F, 256, 128)
    return tm, tf


def _ffn_can_side(s, d_ff):
    fp = -(-d_ff // FFN_TILE_F) * FFN_TILE_F
    tm, tf = _ffn_tiles(s, fp)
    return d_ff % LANES == 0 and (s // tm) * (fp // tf) >= 3 * (fp // LANES)


def _ffn(x, g, w1, w2, gn, *, emit_x, emit_h, h_dtype, side=None):
    s, d = x.shape
    fp = w2.shape[0]
    tm, tf = _ffn_tiles(s, fp)
    nf = fp // tf
    row = lambda i, f: (i, 0)
    in_specs = [
        pl.BlockSpec((tm, d), row),
        pl.BlockSpec((1, d), lambda i, f: (0, 0)),
        pl.BlockSpec((None, d, tf), lambda i, f: (0, 0, f)),
        pl.BlockSpec((None, d, tf), lambda i, f: (1, 0, f)),
        pl.BlockSpec((tf, d), lambda i, f: (f, 0)),
        pl.BlockSpec((1, d), lambda i, f: (0, 0)),
    ]
    operands = [x, g, w1, w1, w2, gn]
    out_shape, out_specs = [], []
    if emit_x:
        out_shape.append(jax.ShapeDtypeStruct((s, d), F32))
        out_specs.append(pl.BlockSpec((tm, d), row))
    if emit_h:
        out_shape.append(jax.ShapeDtypeStruct((s, d), h_dtype))
        out_specs.append(pl.BlockSpec((tm, d), row))
    side_cfg = None
    if side is not None:
        w1f, w2f, layer = side
        d_ff = w2f.shape[1]
        n_slab, n_valid = fp // LANES, d_ff // LANES
        assert _ffn_can_side(s, d_ff)
        side_cfg = (n_slab, n_valid)
        ta = lambda i, f: jnp.minimum(i * nf + f, 2 * n_slab - 1)
        tb = lambda i, f: jnp.clip(i * nf + f - 2 * n_slab, 0, n_slab - 1)
        in_specs += [
            pl.BlockSpec((None, d, LANES), lambda i, f: (
                layer, 0, (ta(i, f) // n_slab) * n_valid + jnp.minimum(ta(i, f) % n_slab, n_valid - 1))),
            pl.BlockSpec((None, LANES, d), lambda i, f: (layer, jnp.minimum(tb(i, f), n_valid - 1), 0)),
        ]
        operands += [w1f, w2f]
        out_shape += [jax.ShapeDtypeStruct((2, d, fp), BF16), jax.ShapeDtypeStruct((fp, d), BF16)]
        out_specs += [
            pl.BlockSpec((None, d, LANES), lambda i, f: (ta(i, f) // n_slab, 0, ta(i, f) % n_slab)),
            pl.BlockSpec((LANES, d), lambda i, f: (tb(i, f), 0)),
        ]
    return pl.pallas_call(
        functools.partial(_ffn_body, emit_x=emit_x, emit_h=emit_h, side=side_cfg),
        grid=(s // tm, nf),
        in_specs=in_specs,
        out_specs=out_specs,
        out_shape=out_shape,
        scratch_shapes=[pltpu.VMEM((tm, d), BF16), pltpu.VMEM((tm, d), F32)],
        compiler_params=_params(2),
        name="ffn",
    )(*operands)


def _proj_a_body(h_ref, wb_ref, wc_ref, wx_ref, cw_ref, o_ref, win_sc, *, halo):
    tm = h_ref.shape[0]
    k_width = cw_ref.shape[0]

    @pl.when(pl.program_id(1) == 0)
    def _():
        win_sc[0:halo, :] = jnp.zeros((halo, win_sc.shape[1]), F32)

    h = h_ref[...]
    win_sc[halo:halo + tm, :] = _dot(h, wc_ref[...]) * _dot(h, wx_ref[...])
    conv = None
    for k in range(k_width):
        term = cw_ref[k:k + 1, :] * win_sc[pl.ds(halo - (k_width - 1 - k), tm), :]
        conv = term if conv is None else conv + term
    o_ref[...] = (_dot(h, wb_ref[...]) * conv).astype(o_ref.dtype)
    win_sc[0:halo, :] = win_sc[tm:tm + halo, :]


def _proj_a(h, w_in, layer, conv_w, d_a):
    s, d = h.shape
    tm = _pick(s, 512, 256, 128)
    tn = _pick(d_a, 512, 256, 128)
    nj = d_a // tn
    halo = SUBLANES
    wspec = lambda off: pl.BlockSpec((None, d, tn), lambda j, i, off=off: (layer, 0, off + j))
    return pl.pallas_call(
        functools.partial(_proj_a_body, halo=halo),
        grid=(nj, s // tm),
        in_specs=[
            pl.BlockSpec((tm, d), lambda j, i: (i, 0)),
            wspec(0), wspec(nj), wspec(2 * nj),
            pl.BlockSpec((None, conv_w.shape[1], tn), lambda j, i: (layer, 0, j)),
        ],
        out_specs=pl.BlockSpec((tm, tn), lambda j, i: (i, j)),
        out_shape=jax.ShapeDtypeStruct((s, d_a), BF16),
        scratch_shapes=[pltpu.VMEM((halo + tm, tn), F32)],
        compiler_params=_params(2),
        name="proj_a",
    )(h, w_in, w_in, w_in, conv_w)


def _halo_reset(win_sc, halo):
    @pl.when(pl.program_id(1) == 0)
    def _():
        win_sc[0:halo, :] = jnp.zeros((halo, win_sc.shape[1]), F32)


def _halo_carry(win_sc, halo, tm):
    win_sc[0:halo, :] = win_sc[tm:tm + halo, :]


def _proj_qk_body(h_ref, w_ref, cw_ref, sc_ref, o_ref, win_sc, *, halo):
    tm, tn = o_ref.shape
    k_width = cw_ref.shape[0]
    rb = _pick(tm, EPI_ROW_BLOCK)
    _halo_reset(win_sc, halo)
    h = h_ref[...]
    pc = _pick(tn, MXU_PIECE, LANES)
    for p0 in range(0, tn, pc):
        ps = slice(p0, p0 + pc)
        win_sc[halo:halo + tm, ps] = _dot(h, w_ref[:, ps])
        for c0 in range(p0, p0 + pc, LANES):
            cs = slice(c0, c0 + LANES)
            for r0 in range(0, tm, rb):
                y = None
                for k in range(k_width):
                    lo = halo + r0 - (k_width - 1 - k)
                    term = cw_ref[k:k + 1, cs] * win_sc[lo:lo + rb, cs]
                    y = term if y is None else y + term
                o_ref[r0:r0 + rb, cs] = (_silu(y) * sc_ref[:, cs]).astype(o_ref.dtype)
    _halo_carry(win_sc, halo, tm)


def _proj_qk(h, w_in, layer, col0, conv_w, scale):
    s, d = h.shape
    k_width, n = conv_w.shape[1:]
    tm = _pick(s, 512, 256, 128)
    tn = _pick(math.gcd(n, col0), 1024, 512, 256, 128)
    halo = SUBLANES
    j0 = col0 // tn
    return pl.pallas_call(
        functools.partial(_proj_qk_body, halo=halo),
        grid=(n // tn, s // tm),
        in_specs=[
            pl.BlockSpec((tm, d), lambda j, i: (i, 0)),
            pl.BlockSpec((None, d, tn), lambda j, i: (layer, 0, j0 + j)),
            pl.BlockSpec((None, k_width, tn), lambda j, i: (layer, 0, j)),
            pl.BlockSpec((1, tn), lambda j, i: (0, j)),
        ],
        out_specs=pl.BlockSpec((tm, tn), lambda j, i: (i, j)),
        out_shape=jax.ShapeDtypeStruct((s, n), BF16),
        scratch_shapes=[pltpu.VMEM((halo + tm, tn), F32)],
        compiler_params=_params(2),
        name="proj_qk",
    )(h, w_in, conv_w, scale)


def _conv_wide(win_sc, cw_ref, cb_ref, z_sc, halo, tm, c_lo, c_hi):
    k_width = cw_ref.shape[0]
    rb = _pick(tm, EPI_ROW_BLOCK)
    for c0 in range(c_lo, c_hi, LANES):
        cs = slice(c0, c0 + LANES)
        for r0 in range(0, tm, rb):
            z = jnp.zeros((rb, LANES), F32) + cb_ref[:, cs]
            for r in range(min(SUBLANES, k_width)):
                y = None
                for j in range(r, k_width, SUBLANES):
                    lo = halo + r0 - SUBLANES - (j - r)
                    term = cw_ref[k_width - 1 - j:k_width - j, cs] * win_sc[lo:lo + rb + SUBLANES, cs]
                    y = term if y is None else y + term
                z = z + y[SUBLANES - r:SUBLANES - r + rb, :]
            z_sc[r0:r0 + rb, cs] = z


def _proj_b_body(h_ref, wv_ref, wg_ref, cw_ref, cb_ref, lg_ref, lb_ref, o_ref, win_sc, z_sc, *, halo):
    tm, n = o_ref.shape
    rb = _pick(tm, EPI_ROW_BLOCK)
    _halo_reset(win_sc, halo)
    h = h_ref[...]
    pc = _pick(n, MXU_PIECE, LANES)
    for p0 in range(0, n, pc):
        ps = slice(p0, p0 + pc)
        win_sc[halo:halo + tm, ps] = _dot(h, wv_ref[:, ps]) * jax.nn.sigmoid(_dot(h, wg_ref[:, ps]))
        _conv_wide(win_sc, cw_ref, cb_ref, z_sc, halo, tm, p0, p0 + pc)
    for r0 in range(0, tm, rb):
        z = z_sc[r0:r0 + rb, :]
        zc = z - jnp.mean(z, axis=-1, keepdims=True)
        y = zc * lax.rsqrt(jnp.mean(zc * zc, axis=-1, keepdims=True) + EPS) * lg_ref[...] + lb_ref[...]
        o_ref[r0:r0 + rb, :] = _silu(y).astype(o_ref.dtype)
    _halo_carry(win_sc, halo, tm)


def _proj_b(h, w_in, layer, conv_w, conv_b, ln_g, ln_b, col0):
    s, d = h.shape
    k_width, d_b = conv_w.shape[1:]
    tm = _pick(s, 256, 128)
    halo = SUBLANES * (1 + (k_width - 1) // SUBLANES)
    assert col0 % d_b == 0
    jb = col0 // d_b
    vec = pl.BlockSpec((1, d_b), lambda j, i: (0, 0))
    return pl.pallas_call(
        functools.partial(_proj_b_body, halo=halo),
        grid=(1, s // tm),
        in_specs=[
            pl.BlockSpec((tm, d), lambda j, i: (i, 0)),
            pl.BlockSpec((None, d, d_b), lambda j, i: (layer, 0, jb)),
            pl.BlockSpec((None, d, d_b), lambda j, i: (layer, 0, jb + 1)),
            pl.BlockSpec((None, k_width, d_b), lambda j, i: (layer, 0, 0)),
            vec, vec, vec,
        ],
        out_specs=pl.BlockSpec((tm, d_b), lambda j, i: (i, 0)),
        out_shape=jax.ShapeDtypeStruct((s, d_b), BF16),
        scratch_shapes=[pltpu.VMEM((halo + tm, d_b), F32), pltpu.VMEM((tm, d_b), F32)],
        compiler_params=_params(2),
        name="proj_b",
    )(h, w_in, w_in, conv_w, conv_b, ln_g, ln_b)


def _proj_plain_body(h_ref, w_ref, o_ref, *, gate):
    h = h_ref[...]
    pc = _pick(o_ref.shape[1], MXU_PIECE, LANES)
    for p0 in range(0, o_ref.shape[1], pc):
        ps = slice(p0, p0 + pc)
        y = _dot(h, w_ref[:, ps])
        if gate:
            y = jax.nn.sigmoid(y)
        o_ref[:, ps] = y.astype(o_ref.dtype)


def _proj_plain(h, w_in, layer, col0, n, *, gate, name):
    s, d = h.shape
    tm = _pick(s, 512, 256, 128)
    tn = _pick(math.gcd(n, col0), 1024, 512, 256, 128)
    j0 = col0 // tn
    return pl.pallas_call(
        functools.partial(_proj_plain_body, gate=gate),
        grid=(n // tn, s // tm),
        in_specs=[
            pl.BlockSpec((tm, d), lambda j, i: (i, 0)),
            pl.BlockSpec((None, d, tn), lambda j, i: (layer, 0, j0 + j)),
        ],
        out_specs=pl.BlockSpec((tm, tn), lambda j, i: (i, j)),
        out_shape=jax.ShapeDtypeStruct((s, n), BF16),
        compiler_params=_params(2),
        name=name,
    )(h, w_in)


def _gates_body(h_ref, w_ref, b_ref, gc_ref, gr_ref, *, n_heads, chunk):
    r = lax.broadcasted_iota(jnp.int32, (chunk, chunk), 0)
    c = lax.broadcasted_iota(jnp.int32, (chunk, chunk), 1)
    tril = (c <= r).astype(F32)
    lane = lax.broadcasted_iota(jnp.int32, (chunk, LANES), 1)
    for t0 in range(0, h_ref.shape[0], chunk):
        g = _dot(h_ref[t0:t0 + chunk, :], w_ref[...]) + b_ref[...]
        lf = jnp.minimum(g, 0.0) - jnp.log1p(jnp.exp(-jnp.abs(g)))
        cum = jnp.dot(tril, lf, precision=lax.Precision.HIGHEST, preferred_element_type=F32)
        out = jnp.where(lane < n_heads, g, jnp.where(lane < 2 * n_heads, cum, 0.0))
        gc_ref[t0:t0 + chunk, :] = out
        gr_ref[:, t0:t0 + chunk] = out.T[0:gr_ref.shape[0], :]


def _gates(h, w_g, b_g, n_heads, chunk):
    s, d = h.shape
    assert 2 * n_heads <= SUBLANES
    tm = _pick(s, 4 * chunk, 2 * chunk, chunk)
    return pl.pallas_call(
        functools.partial(_gates_body, n_heads=n_heads, chunk=chunk),
        grid=(s // tm,),
        in_specs=[
            pl.BlockSpec((tm, d), lambda c: (c, 0)),
            pl.BlockSpec((d, LANES), lambda c: (0, 0)),
            pl.BlockSpec((1, LANES), lambda c: (0, 0)),
        ],
        out_specs=[
            pl.BlockSpec((tm, LANES), lambda c: (c, 0)),
            pl.BlockSpec((SUBLANES, tm), lambda c: (0, c)),
        ],
        out_shape=[jax.ShapeDtypeStruct((s, LANES), F32), jax.ShapeDtypeStruct((SUBLANES, s), F32)],
        compiler_params=_params(1),
        name="gates",
    )(h, w_g, b_g)


def _mlstm_body(q_ref, k_ref, v_ref, o_ref, gc_ref, gr_ref, mh_ref, out_ref, c_sc, n_sc, m_sc,
                *, n_heads):
    chunk = q_ref.shape[0]
    dh = q_ref.shape[1] // n_heads

    @pl.when(pl.program_id(0) == 0)
    def _():
        c_sc[...] = jnp.zeros_like(c_sc)
        n_sc[...] = jnp.zeros_like(n_sc)
        m_sc[...] = jnp.zeros_like(m_sc)

    r = lax.broadcasted_iota(jnp.int32, (chunk, chunk), 0)
    c = lax.broadcasted_iota(jnp.int32, (chunk, chunk), 1)
    causal = c <= r
    gc = gc_ref[...]
    gr = gr_ref[...]
    for hd in range(n_heads):
        sl = slice(hd * dh, (hd + 1) * dh)
        q, k, v = q_ref[:, sl], k_ref[:, sl], v_ref[:, sl]
        ig_c, b_c = gc[:, hd:hd + 1], gc[:, n_heads + hd:n_heads + hd + 1]
        ig_r, b_r = gr[hd:hd + 1, :], gr[n_heads + hd:n_heads + hd + 1, :]
        m_prev = m_sc[hd * SUBLANES:hd * SUBLANES + 1, 0:1]
        c_prev = c_sc[hd]
        n_prev = n_sc[hd * SUBLANES:hd * SUBLANES + 1, :]

        inter = b_c + m_prev
        dmat = jnp.where(causal, b_c - b_r + ig_r, -jnp.inf)
        m_t = jnp.maximum(inter, jnp.max(dmat, axis=-1, keepdims=True))
        w_inter = jnp.exp(inter - m_t)
        s = lax.dot_general(q, k, (((1,), (1,)), ((), ())), preferred_element_type=F32)
        s = s * jnp.exp(dmat - m_t)
        num = w_inter * _dot(q, c_prev.astype(BF16)) + _dot(s.astype(BF16), v)
        qn = jnp.sum(q.astype(F32) * n_prev, axis=-1, keepdims=True)
        den = w_inter * qn + jnp.sum(s, axis=-1, keepdims=True)
        ht = num / jnp.maximum(jnp.abs(den), jnp.exp(-m_t))
        ht = _rms(ht, mh_ref[:, sl])
        out_ref[:, sl] = (o_ref[:, sl].astype(F32) * ht).astype(out_ref.dtype)

        m_new = m_t[chunk - 1:chunk, :]
        b_last = b_c[chunk - 1:chunk, :]
        decay = jnp.exp(b_last + m_prev - m_new)
        kw = k.astype(F32) * jnp.exp(b_last - b_c + ig_c - m_new)
        c_sc[hd] = decay * c_prev + lax.dot_general(
            kw.astype(BF16), v, (((0,), (0,)), ((), ())), preferred_element_type=F32)
        n_sc[hd * SUBLANES:(hd + 1) * SUBLANES, :] = jnp.broadcast_to(
            decay * n_prev + jnp.sum(kw, axis=0, keepdims=True), (SUBLANES, dh))
        m_sc[hd * SUBLANES:(hd + 1) * SUBLANES, :] = jnp.broadcast_to(m_new, (SUBLANES, LANES))


def _mlstm(qk, v, o, gc, gr, mh, n_heads, chunk):
    s, d_c = v.shape
    dh = d_c // n_heads
    blk = pl.BlockSpec((chunk, d_c), lambda c: (c, 0))
    return pl.pallas_call(
        functools.partial(_mlstm_body, n_heads=n_heads),
        grid=(s // chunk,),
        in_specs=[
            blk,
            pl.BlockSpec((chunk, d_c), lambda c: (c, 1)),
            blk, blk,
            pl.BlockSpec((chunk, LANES), lambda c: (c, 0)),
            pl.BlockSpec((SUBLANES, chunk), lambda c: (0, c)),
            pl.BlockSpec((1, d_c), lambda c: (0, 0)),
        ],
        out_specs=blk,
        out_shape=jax.ShapeDtypeStruct((s, d_c), BF16),
        scratch_shapes=[
            pltpu.VMEM((n_heads, dh, dh), F32),
            pltpu.VMEM((n_heads * SUBLANES, dh), F32),
            pltpu.VMEM((n_heads * SUBLANES, LANES), F32),
        ],
        compiler_params=_params(1),
        name="mlstm",
    )(qk, qk, v, o, gc, gr, mh)


def _mix_body(pa_ref, pb_ref, pc_ref, h_ref, wa_ref, wb_ref, wc_ref, wga_ref, wgb_ref, wgc_ref,
              ba_ref, bb_ref, bc_ref, o_ref):
    h = h_ref[...]
    acc = jax.nn.sigmoid(_dot(h, wga_ref[...]) + ba_ref[...]) * _dot(pa_ref[...], wa_ref[...])
    acc += jax.nn.sigmoid(_dot(h, wgb_ref[...]) + bb_ref[...]) * _dot(pb_ref[...], wb_ref[...])
    acc += jax.nn.sigmoid(_dot(h, wgc_ref[...]) + bc_ref[...]) * _dot(pc_ref[...], wc_ref[...])
    o_ref[...] = acc.astype(o_ref.dtype)


def _mix(pa, pb, pc, h, wa, wb, wc, wg, bg, layer):
    s, d = h.shape
    tm = _pick(s, 512, 256, 128)
    tn = _pick(d, 512, 256, 128)
    nj = d // tn
    rows = lambda n: pl.BlockSpec((tm, n), lambda j, i: (i, 0))
    cols = lambda kdim, off: pl.BlockSpec((None, kdim, tn), lambda j, i, off=off: (layer, 0, off + j))
    return pl.pallas_call(
        _mix_body,
        grid=(nj, s // tm),
        in_specs=[
            rows(pa.shape[1]), rows(pb.shape[1]), rows(pc.shape[1]), rows(d),
            cols(wa.shape[1], 0), cols(wb.shape[1], 0), cols(wc.shape[1], 0),
            cols(d, 0), cols(d, nj), cols(d, 2 * nj),
            cols(1, 0), cols(1, nj), cols(1, 2 * nj),
        ],
        out_specs=pl.BlockSpec((tm, tn), lambda j, i: (i, j)),
        out_shape=jax.ShapeDtypeStruct((s, d), BF16),
        compiler_params=_params(2),
        name="mix",
    )(pa, pb, pc, h, wa, wb, wc, wg, wg, wg, bg, bg, bg)


def _out_body(x_ref, m_ref, w_ref, o_ref):
    o_ref[...] = x_ref[...] + _dot(m_ref[...], w_ref[...])


def _out(x, mix, w_o, layer):
    s, d = x.shape
    tm = _pick(s, 512, 256, 128)
    row = pl.BlockSpec((tm, d), lambda i: (i, 0))
    return pl.pallas_call(
        _out_body,
        grid=(s // tm,),
        in_specs=[row, row, pl.BlockSpec((None, d, d), lambda i: (layer, 0, 0))],
        out_specs=row,
        out_shape=jax.ShapeDtypeStruct((s, d), F32),
        compiler_params=_params(1),
        name="out_proj",
    )(x, mix, w_o)


def _cast_w1_body(*refs, n_valid):
    ins, o_ref = refs[:-1], refs[-1]
    c = pl.program_id(1)
    for q, x_ref in enumerate(ins):
        valid = c * len(ins) + q < n_valid
        o_ref[:, q * LANES:(q + 1) * LANES] = jnp.where(valid, x_ref[...], 0.0).astype(o_ref.dtype)


def _cast_w2_body(*refs, n_valid):
    ins, o_ref = refs[:-1], refs[-1]
    c = pl.program_id(0)
    for q, x_ref in enumerate(ins):
        valid = c * len(ins) + q < n_valid
        o_ref[q * LANES:(q + 1) * LANES, :] = jnp.where(valid, x_ref[...], 0.0).astype(o_ref.dtype)


def _ffn_weights(w1, w2, layer):
    _, d, _ = w1.shape
    d_ff = w2.shape[1]
    assert d_ff % LANES == 0
    nb = d_ff // LANES
    per = FFN_TILE_F // LANES
    fp = -(-d_ff // FFN_TILE_F) * FFN_TILE_F
    w1_specs = [pl.BlockSpec((None, d, LANES),
                             lambda g, c, q=q: (layer, 0, g * nb + jnp.minimum(c * per + q, nb - 1)))
                for q in range(per)]
    w1b = pl.pallas_call(
        functools.partial(_cast_w1_body, n_valid=nb),
        grid=(2, fp // FFN_TILE_F),
        in_specs=w1_specs,
        out_specs=pl.BlockSpec((None, d, FFN_TILE_F), lambda g, c: (g, 0, c)),
        out_shape=jax.ShapeDtypeStruct((2, d, fp), BF16),
        compiler_params=_params(2),
        name="cast_w1",
    )(*([w1] * per))
    w2_specs = [pl.BlockSpec((None, LANES, d), lambda c, q=q: (layer, jnp.minimum(c * per + q, nb - 1), 0))
                for q in range(per)]
    w2b = pl.pallas_call(
        functools.partial(_cast_w2_body, n_valid=nb),
        grid=(fp // FFN_TILE_F,),
        in_specs=w2_specs,
        out_specs=pl.BlockSpec((FFN_TILE_F, d), lambda c: (c, 0)),
        out_shape=jax.ShapeDtypeStruct((fp, d), BF16),
        compiler_params=_params(1),
        name="cast_w2",
    )(*([w2] * per))
    return w1b, w2b


def kernel(x, ffn1_norm, ffn1_w1, ffn1_w2, mix_norm, w_in, conv_a_w, w_out_a, conv_b_w, conv_b_bias,
           ln_b_gain, ln_b_bias, w_out_b, conv_c_w, ig_bias, fg_bias, mh_norm, w_out_c, w_gate, b_gate,
           w_o, ffn2_norm, ffn2_w1, ffn2_w2, final_norm):
    bsz, s, d = x.shape
    depth = ffn1_norm.shape[0]
    d_a, d_b, d_c = conv_a_w.shape[-1], conv_b_w.shape[-1], mh_norm.shape[-1]
    n_heads = ig_bias.shape[-1]
    chunk = _pick(s, MLSTM_CHUNK)
    col_b = 3 * d_a
    col_q = col_b + 2 * d_b
    col_v = col_q + 2 * d_c
    col_o = col_v + d_c
    col_g = col_o + d_c
    vec = lambda a: a.reshape(1, -1).astype(F32)
    qk_scale = jnp.concatenate(
        [jnp.full((1, d_c), (d_c // n_heads) ** -0.5, F32), jnp.ones((1, d_c), F32)], axis=-1)

    side_ok = _ffn_can_side(s, ffn1_w2.shape[1])
    w_in_b = w_in.astype(BF16)
    wa_b, wb_b, wc_b = w_out_a.astype(BF16), w_out_b.astype(BF16), w_out_c.astype(BF16)
    wg_b, wo_b = w_gate.astype(BF16), w_o.astype(BF16)
    bg3 = b_gate.reshape(depth, 1, -1).astype(F32)
    gate_pad = ((0, 0), (0, LANES - 2 * n_heads))

    outs = []
    for b in range(bsz):
        xb = x[b]
        f1w = _ffn_weights(ffn1_w1, ffn1_w2, 0)
        for l in range(depth):
            res = _ffn(xb, vec(ffn1_norm[l]), *f1w, vec(mix_norm[l]), emit_x=True, emit_h=True, h_dtype=BF16,
                       side=(ffn2_w1, ffn2_w2, l) if side_ok else None)
            xb, h = res[0], res[1]
            f2w = res[2:] if side_ok else _ffn_weights(ffn2_w1, ffn2_w2, l)

            pa = _proj_a(h, w_in_b, l, conv_a_w, d_a)
            pb = _proj_b(h, w_in_b, l, conv_b_w, vec(conv_b_bias[l]), vec(ln_b_gain[l]),
                         vec(ln_b_bias[l]), col_b)
            qk = _proj_qk(h, w_in_b, l, col_q, conv_c_w, qk_scale)
            v = _proj_plain(h, w_in_b, l, col_v, d_c, gate=False, name="proj_v")
            o = _proj_plain(h, w_in_b, l, col_o, d_c, gate=True, name="proj_o")
            w_g = jnp.pad(w_in_b[l, :, col_g:], gate_pad)
            b_g = jnp.pad(jnp.concatenate([ig_bias[l], fg_bias[l]]).reshape(1, -1).astype(F32), gate_pad)
            gc, gr = _gates(h, w_g, b_g, n_heads, chunk)
            pc = _mlstm(qk, v, o, gc, gr, vec(mh_norm[l]), n_heads, chunk)

            mix = _mix(pa, pb, pc, h, wa_b, wb_b, wc_b, wg_b, bg3, l)
            xb = _out(xb, mix, wo_b, l)

            last = l == depth - 1
            res = _ffn(xb, vec(ffn2_norm[l]), *f2w, vec(final_norm), emit_x=not last, emit_h=last, h_dtype=F32,
                       side=(ffn1_w1, ffn1_w2, l + 1) if side_ok and not last else None)
            xb = res[0]
            if not last:
                f1w = res[1:] if side_ok else _ffn_weights(ffn1_w1, ffn1_w2, l + 1)
        outs.append(xb)
    return outs[0][None] if bsz == 1 else jnp.stack(outs, axis=0)
```

```python
import functools
import math

import jax
import jax.numpy as jnp
from jax import lax
from jax.experimental import pallas as pl
from jax.experimental.pallas import tpu as pltpu

EPS = 1e-6
F32 = jnp.float32
BF16 = jnp.bfloat16

LANES = 128
SUBLANES = 8
VMEM_LIMIT_BYTES = 56 * 1024 * 1024

MLSTM_CHUNK = 256
EPI_ROW_BLOCK = 64
FFN_TILE_F = 512
MXU_PIECE = 256


def _pick(n, *cands):
    for c in cands:
        if c <= n and n % c == 0:
            return c
    return n


def _params(n_axes):
    return pltpu.CompilerParams(
        dimension_semantics=("arbitrary",) * n_axes, vmem_limit_bytes=VMEM_LIMIT_BYTES)


def _dot(a, b):
    return jnp.dot(a, b, preferred_element_type=F32)


def _rms(x, g):
    return x * lax.rsqrt(jnp.mean(x * x, axis=-1, keepdims=True) + EPS) * g


def _silu(x):
    return x * jax.nn.sigmoid(x)


def _cast_slab(x_ref, o_ref, valid):
    o_ref[...] = jnp.where(valid, x_ref[...], 0.0).astype(o_ref.dtype)


def _ffn_body(x_ref, g_ref, w1g_ref, w1u_ref, w2_ref, gn_ref, *refs, emit_x, emit_h, side):
    n_side_in = 2 if side else 0
    side_in, refs = refs[:n_side_in], refs[n_side_in:]
    n_out = int(emit_x) + int(emit_h)
    outs, side_out = list(refs[:n_out]), refs[n_out:n_out + n_side_in]
    h_sc, acc_sc = refs[n_out + n_side_in:]
    f = pl.program_id(1)

    @pl.when(f == 0)
    def _():
        h_sc[...] = _rms(x_ref[...], g_ref[...]).astype(BF16)
        acc_sc[...] = jnp.zeros_like(acc_sc)

    h = h_sc[...]
    gate = _dot(h, w1g_ref[...])
    up = _dot(h, w1u_ref[...])
    p = (_silu(gate) * up).astype(BF16)
    acc_sc[...] += _dot(p, w2_ref[...])

    @pl.when(f == pl.num_programs(1) - 1)
    def _():
        xn = x_ref[...] + 0.5 * acc_sc[...]
        if emit_x:
            outs[0][...] = xn
        if emit_h:
            outs[-1][...] = _rms(xn, gn_ref[...]).astype(outs[-1].dtype)

    if side:
        n_slab, n_valid = side
        t = pl.program_id(0) * pl.num_programs(1) + f

        @pl.when(t < 2 * n_slab)
        def _():
            _cast_slab(side_in[0], side_out[0], t % n_slab < n_valid)

        @pl.when((t >= 2 * n_slab) & (t < 3 * n_slab))
        def _():
            _cast_slab(side_in[1], side_out[1], t - 2 * n_slab < n_valid)


def _ffn_tiles(s, fp):
    tm = _pick(s, 512, 256, 128)
    tf = _pick(fp, FFN_TILE_F, 256, 128)
    return tm, tf


def _ffn_can_side(s, d_ff):
    fp = -(-d_ff // FFN_TILE_F) * FFN_TILE_F
    tm, tf = _ffn_tiles(s, fp)
    return d_ff % LANES == 0 and (s // tm) * (fp // tf) >= 3 * (fp // LANES)


def _ffn(x, g, w1, w2, gn, *, emit_x, emit_h, h_dtype, side=None):
    s, d = x.shape
    fp = w2.shape[0]
    tm, tf = _ffn_tiles(s, fp)
    nf = fp // tf
    row = lambda i, f: (i, 0)
    in_specs = [
        pl.BlockSpec((tm, d), row),
        pl.BlockSpec((1, d), lambda i, f: (0, 0)),
        pl.BlockSpec((None, d, tf), lambda i, f: (0, 0, f)),
        pl.BlockSpec((None, d, tf), lambda i, f: (1, 0, f)),
        pl.BlockSpec((tf, d), lambda i, f: (f, 0)),
        pl.BlockSpec((1, d), lambda i, f: (0, 0)),
    ]
    operands = [x, g, w1, w1, w2, gn]
    out_shape, out_specs = [], []
    if emit_x:
        out_shape.append(jax.ShapeDtypeStruct((s, d), F32))
        out_specs.append(pl.BlockSpec((tm, d), row))
    if emit_h:
        out_shape.append(jax.ShapeDtypeStruct((s, d), h_dtype))
        out_specs.append(pl.BlockSpec((tm, d), row))
    side_cfg = None
    if side is not None:
        w1f, w2f, layer = side
        d_ff = w2f.shape[1]
        n_slab, n_valid = fp // LANES, d_ff // LANES
        assert _ffn_can_side(s, d_ff)
        side_cfg = (n_slab, n_valid)
        ta = lambda i, f: jnp.minimum(i * nf + f, 2 * n_slab - 1)
        tb = lambda i, f: jnp.clip(i * nf + f - 2 * n_slab, 0, n_slab - 1)
        in_specs += [
            pl.BlockSpec((None, d, LANES), lambda i, f: (
                layer, 0, (ta(i, f) // n_slab) * n_valid + jnp.minimum(ta(i, f) % n_slab, n_valid - 1))),
            pl.BlockSpec((None, LANES, d), lambda i, f: (layer, jnp.minimum(tb(i, f), n_valid - 1), 0)),
        ]
        operands += [w1f, w2f]
        out_shape += [jax.ShapeDtypeStruct((2, d, fp), BF16), jax.ShapeDtypeStruct((fp, d), BF16)]
        out_specs += [
            pl.BlockSpec((None, d, LANES), lambda i, f: (ta(i, f) // n_slab, 0, ta(i, f) % n_slab)),
            pl.BlockSpec((LANES, d), lambda i, f: (tb(i, f), 0)),
        ]
    return pl.pallas_call(
        functools.partial(_ffn_body, emit_x=emit_x, emit_h=emit_h, side=side_cfg),
        grid=(s // tm, nf),
        in_specs=in_specs,
        out_specs=out_specs,
        out_shape=out_shape,
        scratch_shapes=[pltpu.VMEM((tm, d), BF16), pltpu.VMEM((tm, d), F32)],
        compiler_params=_params(2),
        name="ffn",
    )(*operands)


def _proj_a_body(h_ref, wb_ref, wc_ref, wx_ref, cw_ref, o_ref, win_sc, *, halo):
    tm = h_ref.shape[0]
    k_width = cw_ref.shape[0]

    @pl.when(pl.program_id(1) == 0)
    def _():
        win_sc[0:halo, :] = jnp.zeros((halo, win_sc.shape[1]), F32)

    h = h_ref[...]
    win_sc[halo:halo + tm, :] = _dot(h, wc_ref[...]) * _dot(h, wx_ref[...])
    conv = None
    for k in range(k_width):
        term = cw_ref[k:k + 1, :] * win_sc[pl.ds(halo - (k_width - 1 - k), tm), :]
        conv = term if conv is None else conv + term
    o_ref[...] = (_dot(h, wb_ref[...]) * conv).astype(o_ref.dtype)
    win_sc[0:halo, :] = win_sc[tm:tm + halo, :]


def _proj_a(h, w_in, layer, conv_w, d_a):
    s, d = h.shape
    tm = _pick(s, 512, 256, 128)
    tn = _pick(d_a, 512, 256, 128)
    nj = d_a // tn
    halo = SUBLANES
    wspec = lambda off: pl.BlockSpec((None, d, tn), lambda j, i, off=off: (layer, 0, off + j))
    return pl.pallas_call(
        functools.partial(_proj_a_body, halo=halo),
        grid=(nj, s // tm),
        in_specs=[
            pl.BlockSpec((tm, d), lambda j, i: (i, 0)),
            wspec(0), wspec(nj), wspec(2 * nj),
            pl.BlockSpec((None, conv_w.shape[1], tn), lambda j, i: (layer, 0, j)),
        ],
        out_specs=pl.BlockSpec((tm, tn), lambda j, i: (i, j)),
        out_shape=jax.ShapeDtypeStruct((s, d_a), BF16),
        scratch_shapes=[pltpu.VMEM((halo + tm, tn), F32)],
        compiler_params=_params(2),
        name="proj_a",
    )(h, w_in, w_in, w_in, conv_w)


def _halo_reset(win_sc, halo):
    @pl.when(pl.program_id(1) == 0)
    def _():
        win_sc[0:halo, :] = jnp.zeros((halo, win_sc.shape[1]), F32)


def _halo_carry(win_sc, halo, tm):
    win_sc[0:halo, :] = win_sc[tm:tm + halo, :]


def _shift_rows(a, j):
    n = a.shape[0] // SUBLANES
    if j == 0:
        return a[SUBLANES:, :]
    rolled = [pltpu.roll(a[p * SUBLANES:(p + 1) * SUBLANES, :], j, axis=0) for p in range(n)]
    from_prev = lax.broadcasted_iota(jnp.int32, (SUBLANES, a.shape[1]), 0) < j
    return jnp.concatenate([jnp.where(from_prev, rolled[m], rolled[m + 1]) for m in range(n - 1)], axis=0)


def _proj_qk_body(h_ref, w_ref, cw_ref, sc_ref, o_ref, win_sc, *, halo):
    tm, tn = o_ref.shape
    k_width = cw_ref.shape[0]
    rb = _pick(tm, EPI_ROW_BLOCK)
    _halo_reset(win_sc, halo)
    h = h_ref[...]
    pc = _pick(tn, MXU_PIECE, LANES)
    for p0 in range(0, tn, pc):
        ps = slice(p0, p0 + pc)
        win_sc[halo:halo + tm, ps] = _dot(h, w_ref[:, ps])
        for c0 in range(p0, p0 + pc, LANES):
            cs = slice(c0, c0 + LANES)
            for r0 in range(0, tm, rb):
                lo = halo + r0 - SUBLANES
                a = win_sc[lo:lo + rb + SUBLANES, cs]
                y = None
                for j in range(k_width):
                    term = cw_ref[k_width - 1 - j:k_width - j, cs] * _shift_rows(a, j)
                    y = term if y is None else y + term
                o_ref[r0:r0 + rb, cs] = (_silu(y) * sc_ref[:, cs]).astype(o_ref.dtype)
    _halo_carry(win_sc, halo, tm)


def _proj_qk(h, w_in, layer, col0, conv_w, scale):
    s, d = h.shape
    k_width, n = conv_w.shape[1:]
    tm = _pick(s, 512, 256, 128)
    tn = _pick(math.gcd(n, col0), 1024, 512, 256, 128)
    halo = SUBLANES
    j0 = col0 // tn
    return pl.pallas_call(
        functools.partial(_proj_qk_body, halo=halo),
        grid=(n // tn, s // tm),
        in_specs=[
            pl.BlockSpec((tm, d), lambda j, i: (i, 0)),
            pl.BlockSpec((None, d, tn), lambda j, i: (layer, 0, j0 + j)),
            pl.BlockSpec((None, k_width, tn), lambda j, i: (layer, 0, j)),
            pl.BlockSpec((1, tn), lambda j, i: (0, j)),
        ],
        out_specs=pl.BlockSpec((tm, tn), lambda j, i: (i, j)),
        out_shape=jax.ShapeDtypeStruct((s, n), BF16),
        scratch_shapes=[pltpu.VMEM((halo + tm, tn), F32)],
        compiler_params=_params(2),
        name="proj_qk",
    )(h, w_in, conv_w, scale)


def _conv_wide(win_sc, cw_ref, cb_ref, z_sc, halo, tm, c_lo, c_hi):
    k_width = cw_ref.shape[0]
    rb = _pick(tm, 2 * EPI_ROW_BLOCK, EPI_ROW_BLOCK)
    for c0 in range(c_lo, c_hi, LANES):
        cs = slice(c0, c0 + LANES)
        for r0 in range(0, tm, rb):
            z = jnp.zeros((rb, LANES), F32) + cb_ref[:, cs]
            for r in range(min(SUBLANES, k_width)):
                y = None
                for j in range(r, k_width, SUBLANES):
                    lo = halo + r0 - SUBLANES - (j - r)
                    term = cw_ref[k_width - 1 - j:k_width - j, cs] * win_sc[lo:lo + rb + SUBLANES, cs]
                    y = term if y is None else y + term
                z = z + _shift_rows(y, r)
            z_sc[r0:r0 + rb, cs] = z


def _proj_b_body(h_ref, wv_ref, wg_ref, cw_ref, cb_ref, lg_ref, lb_ref, o_ref, win_sc, z_sc, *, halo):
    tm, n = o_ref.shape
    rb = _pick(tm, EPI_ROW_BLOCK)
    _halo_reset(win_sc, halo)
    h = h_ref[...]
    pc = _pick(n, MXU_PIECE, LANES)
    for p0 in range(0, n, pc):
        ps = slice(p0, p0 + pc)
        win_sc[halo:halo + tm, ps] = _dot(h, wv_ref[:, ps]) * jax.nn.sigmoid(_dot(h, wg_ref[:, ps]))
        _conv_wide(win_sc, cw_ref, cb_ref, z_sc, halo, tm, p0, p0 + pc)
    for r0 in range(0, tm, rb):
        z = z_sc[r0:r0 + rb, :]
        zc = z - jnp.mean(z, axis=-1, keepdims=True)
        y = zc * lax.rsqrt(jnp.mean(zc * zc, axis=-1, keepdims=True) + EPS) * lg_ref[...] + lb_ref[...]
        o_ref[r0:r0 + rb, :] = _silu(y).astype(o_ref.dtype)
    _halo_carry(win_sc, halo, tm)


def _proj_b(h, w_in, layer, conv_w, conv_b, ln_g, ln_b, col0):
    s, d = h.shape
    k_width, d_b = conv_w.shape[1:]
    tm = _pick(s, 256, 128)
    halo = SUBLANES * (1 + (k_width - 1) // SUBLANES)
    assert col0 % d_b == 0
    jb = col0 // d_b
    vec = pl.BlockSpec((1, d_b), lambda j, i: (0, 0))
    return pl.pallas_call(
        functools.partial(_proj_b_body, halo=halo),
        grid=(1, s // tm),
        in_specs=[
            pl.BlockSpec((tm, d), lambda j, i: (i, 0)),
            pl.BlockSpec((None, d, d_b), lambda j, i: (layer, 0, jb)),
            pl.BlockSpec((None, d, d_b), lambda j, i: (layer, 0, jb + 1)),
            pl.BlockSpec((None, k_width, d_b), lambda j, i: (layer, 0, 0)),
            vec, vec, vec,
        ],
        out_specs=pl.BlockSpec((tm, d_b), lambda j, i: (i, 0)),
        out_shape=jax.ShapeDtypeStruct((s, d_b), BF16),
        scratch_shapes=[pltpu.VMEM((halo + tm, d_b), F32), pltpu.VMEM((tm, d_b), F32)],
        compiler_params=_params(2),
        name="proj_b",
    )(h, w_in, w_in, conv_w, conv_b, ln_g, ln_b)


def _proj_plain_body(h_ref, w_ref, o_ref, *, gate):
    h = h_ref[...]
    pc = _pick(o_ref.shape[1], MXU_PIECE, LANES)
    for p0 in range(0, o_ref.shape[1], pc):
        ps = slice(p0, p0 + pc)
        y = _dot(h, w_ref[:, ps])
        if gate:
            y = jax.nn.sigmoid(y)
        o_ref[:, ps] = y.astype(o_ref.dtype)


def _proj_plain(h, w_in, layer, col0, n, *, gate, name):
    s, d = h.shape
    tm = _pick(s, 512, 256, 128)
    tn = _pick(math.gcd(n, col0), 1024, 512, 256, 128)
    j0 = col0 // tn
    return pl.pallas_call(
        functools.partial(_proj_plain_body, gate=gate),
        grid=(n // tn, s // tm),
        in_specs=[
            pl.BlockSpec((tm, d), lambda j, i: (i, 0)),
            pl.BlockSpec((None, d, tn), lambda j, i: (layer, 0, j0 + j)),
        ],
        out_specs=pl.BlockSpec((tm, tn), lambda j, i: (i, j)),
        out_shape=jax.ShapeDtypeStruct((s, n), BF16),
        compiler_params=_params(2),
        name=name,
    )(h, w_in)


def _gates_body(h_ref, w_ref, b_ref, gc_ref, gr_ref, *, n_heads, chunk):
    r = lax.broadcasted_iota(jnp.int32, (chunk, chunk), 0)
    c = lax.broadcasted_iota(jnp.int32, (chunk, chunk), 1)
    tril = (c <= r).astype(F32)
    lane = lax.broadcasted_iota(jnp.int32, (chunk, LANES), 1)
    for t0 in range(0, h_ref.shape[0], chunk):
        g = _dot(h_ref[t0:t0 + chunk, :], w_ref[...]) + b_ref[...]
        lf = jnp.minimum(g, 0.0) - jnp.log1p(jnp.exp(-jnp.abs(g)))
        cum = jnp.dot(tril, lf, precision=lax.Precision.HIGHEST, preferred_element_type=F32)
        out = jnp.where(lane < n_heads, g, jnp.where(lane < 2 * n_heads, cum, 0.0))
        gc_ref[t0:t0 + chunk, :] = out
        gr_ref[:, t0:t0 + chunk] = out.T[0:gr_ref.shape[0], :]


def _gates(h, w_g, b_g, n_heads, chunk):
    s, d = h.shape
    assert 2 * n_heads <= SUBLANES
    tm = _pick(s, 4 * chunk, 2 * chunk, chunk)
    return pl.pallas_call(
        functools.partial(_gates_body, n_heads=n_heads, chunk=chunk),
        grid=(s // tm,),
        in_specs=[
            pl.BlockSpec((tm, d), lambda c: (c, 0)),
            pl.BlockSpec((d, LANES), lambda c: (0, 0)),
            pl.BlockSpec((1, LANES), lambda c: (0, 0)),
        ],
        out_specs=[
            pl.BlockSpec((tm, LANES), lambda c: (c, 0)),
            pl.BlockSpec((SUBLANES, tm), lambda c: (0, c)),
        ],
        out_shape=[jax.ShapeDtypeStruct((s, LANES), F32), jax.ShapeDtypeStruct((SUBLANES, s), F32)],
        compiler_params=_params(1),
        name="gates",
    )(h, w_g, b_g)


def _mlstm_body(q_ref, k_ref, v_ref, o_ref, gc_ref, gr_ref, mh_ref, out_ref, c_sc, n_sc, m_sc,
                *, n_heads):
    chunk = q_ref.shape[0]
    dh = q_ref.shape[1] // n_heads

    @pl.when(pl.program_id(0) == 0)
    def _():
        c_sc[...] = jnp.zeros_like(c_sc)
        n_sc[...] = jnp.zeros_like(n_sc)
        m_sc[...] = jnp.zeros_like(m_sc)

    r = lax.broadcasted_iota(jnp.int32, (chunk, chunk), 0)
    c = lax.broadcasted_iota(jnp.int32, (chunk, chunk), 1)
    causal = c <= r
    gc = gc_ref[...]
    gr = gr_ref[...]
    for hd in range(n_heads):
        sl = slice(hd * dh, (hd + 1) * dh)
        q, k, v = q_ref[:, sl], k_ref[:, sl], v_ref[:, sl]
        ig_c, b_c = gc[:, hd:hd + 1], gc[:, n_heads + hd:n_heads + hd + 1]
        ig_r, b_r = gr[hd:hd + 1, :], gr[n_heads + hd:n_heads + hd + 1, :]
        m_prev = m_sc[hd * SUBLANES:hd * SUBLANES + 1, 0:1]
        c_prev = c_sc[hd]
        n_prev = n_sc[hd * SUBLANES:hd * SUBLANES + 1, :]

        inter = b_c + m_prev
        dmat = jnp.where(causal, b_c - b_r + ig_r, -jnp.inf)
        m_t = jnp.maximum(inter, jnp.max(dmat, axis=-1, keepdims=True))
        w_inter = jnp.exp(inter - m_t)
        s = lax.dot_general(q, k, (((1,), (1,)), ((), ())), preferred_element_type=F32)
        s = s * jnp.exp(dmat - m_t)
        num = w_inter * _dot(q, c_prev.astype(BF16)) + _dot(s.astype(BF16), v)
        qn = jnp.sum(q.astype(F32) * n_prev, axis=-1, keepdims=True)
        den = w_inter * qn + jnp.sum(s, axis=-1, keepdims=True)
        ht = num / jnp.maximum(jnp.abs(den), jnp.exp(-m_t))
        ht = _rms(ht, mh_ref[:, sl])
        out_ref[:, sl] = (o_ref[:, sl].astype(F32) * ht).astype(out_ref.dtype)

        m_new = m_t[chunk - 1:chunk, :]
        b_last = b_c[chunk - 1:chunk, :]
        decay = jnp.exp(b_last + m_prev - m_new)
        kw = k.astype(F32) * jnp.exp(b_last - b_c + ig_c - m_new)
        c_sc[hd] = decay * c_prev + lax.dot_general(
            kw.astype(BF16), v, (((0,), (0,)), ((), ())), preferred_element_type=F32)
        n_sc[hd * SUBLANES:(hd + 1) * SUBLANES, :] = jnp.broadcast_to(
            decay * n_prev + jnp.sum(kw, axis=0, keepdims=True), (SUBLANES, dh))
        m_sc[hd * SUBLANES:(hd + 1) * SUBLANES, :] = jnp.broadcast_to(m_new, (SUBLANES, LANES))


def _mlstm(qk, v, o, gc, gr, mh, n_heads, chunk):
    s, d_c = v.shape
    dh = d_c // n_heads
    blk = pl.BlockSpec((chunk, d_c), lambda c: (c, 0))
    return pl.pallas_call(
        functools.partial(_mlstm_body, n_heads=n_heads),
        grid=(s // chunk,),
        in_specs=[
            blk,
            pl.BlockSpec((chunk, d_c), lambda c: (c, 1)),
            blk, blk,
            pl.BlockSpec((chunk, LANES), lambda c: (c, 0)),
            pl.BlockSpec((SUBLANES, chunk), lambda c: (0, c)),
            pl.BlockSpec((1, d_c), lambda c: (0, 0)),
        ],
        out_specs=blk,
        out_shape=jax.ShapeDtypeStruct((s, d_c), BF16),
        scratch_shapes=[
            pltpu.VMEM((n_heads, dh, dh), F32),
            pltpu.VMEM((n_heads * SUBLANES, dh), F32),
            pltpu.VMEM((n_heads * SUBLANES, LANES), F32),
        ],
        compiler_params=_params(1),
        name="mlstm",
    )(qk, qk, v, o, gc, gr, mh)


def _mix_body(pa_ref, pb_ref, pc_ref, h_ref, wa_ref, wb_ref, wc_ref, wga_ref, wgb_ref, wgc_ref,
              ba_ref, bb_ref, bc_ref, o_ref):
    h = h_ref[...]
    acc = jax.nn.sigmoid(_dot(h, wga_ref[...]) + ba_ref[...]) * _dot(pa_ref[...], wa_ref[...])
    acc += jax.nn.sigmoid(_dot(h, wgb_ref[...]) + bb_ref[...]) * _dot(pb_ref[...], wb_ref[...])
    acc += jax.nn.sigmoid(_dot(h, wgc_ref[...]) + bc_ref[...]) * _dot(pc_ref[...], wc_ref[...])
    o_ref[...] = acc.astype(o_ref.dtype)


def _mix(pa, pb, pc, h, wa, wb, wc, wg, bg, layer):
    s, d = h.shape
    tm = _pick(s, 512, 256, 128)
    tn = _pick(d, 512, 256, 128)
    nj = d // tn
    rows = lambda n: pl.BlockSpec((tm, n), lambda j, i: (i, 0))
    cols = lambda kdim, off: pl.BlockSpec((None, kdim, tn), lambda j, i, off=off: (layer, 0, off + j))
    return pl.pallas_call(
        _mix_body,
        grid=(nj, s // tm),
        in_specs=[
            rows(pa.shape[1]), rows(pb.shape[1]), rows(pc.shape[1]), rows(d),
            cols(wa.shape[1], 0), cols(wb.shape[1], 0), cols(wc.shape[1], 0),
            cols(d, 0), cols(d, nj), cols(d, 2 * nj),
            cols(1, 0), cols(1, nj), cols(1, 2 * nj),
        ],
        out_specs=pl.BlockSpec((tm, tn), lambda j, i: (i, j)),
        out_shape=jax.ShapeDtypeStruct((s, d), BF16),
        compiler_params=_params(2),
        name="mix",
    )(pa, pb, pc, h, wa, wb, wc, wg, wg, wg, bg, bg, bg)


def _out_body(x_ref, m_ref, w_ref, o_ref):
    o_ref[...] = x_ref[...] + _dot(m_ref[...], w_ref[...])


def _out(x, mix, w_o, layer):
    s, d = x.shape
    tm = _pick(s, 512, 256, 128)
    row = pl.BlockSpec((tm, d), lambda i: (i, 0))
    return pl.pallas_call(
        _out_body,
        grid=(s // tm,),
        in_specs=[row, row, pl.BlockSpec((None, d, d), lambda i: (layer, 0, 0))],
        out_specs=row,
        out_shape=jax.ShapeDtypeStruct((s, d), F32),
        compiler_params=_params(1),
        name="out_proj",
    )(x, mix, w_o)


def _cast_w1_body(*refs, n_valid):
    ins, o_ref = refs[:-1], refs[-1]
    c = pl.program_id(1)
    for q, x_ref in enumerate(ins):
        valid = c * len(ins) + q < n_valid
        o_ref[:, q * LANES:(q + 1) * LANES] = jnp.where(valid, x_ref[...], 0.0).astype(o_ref.dtype)


def _cast_w2_body(*refs, n_valid):
    ins, o_ref = refs[:-1], refs[-1]
    c = pl.program_id(0)
    for q, x_ref in enumerate(ins):
        valid = c * len(ins) + q < n_valid
        o_ref[q * LANES:(q + 1) * LANES, :] = jnp.where(valid, x_ref[...], 0.0).astype(o_ref.dtype)


def _ffn_weights(w1, w2, layer):
    _, d, _ = w1.shape
    d_ff = w2.shape[1]
    assert d_ff % LANES == 0
    nb = d_ff // LANES
    per = FFN_TILE_F // LANES
    fp = -(-d_ff // FFN_TILE_F) * FFN_TILE_F
    w1_specs = [pl.BlockSpec((None, d, LANES),
                             lambda g, c, q=q: (layer, 0, g * nb + jnp.minimum(c * per + q, nb - 1)))
                for q in range(per)]
    w1b = pl.pallas_call(
        functools.partial(_cast_w1_body, n_valid=nb),
        grid=(2, fp // FFN_TILE_F),
        in_specs=w1_specs,
        out_specs=pl.BlockSpec((None, d, FFN_TILE_F), lambda g, c: (g, 0, c)),
        out_shape=jax.ShapeDtypeStruct((2, d, fp), BF16),
        compiler_params=_params(2),
        name="cast_w1",
    )(*([w1] * per))
    w2_specs = [pl.BlockSpec((None, LANES, d), lambda c, q=q: (layer, jnp.minimum(c * per + q, nb - 1), 0))
                for q in range(per)]
    w2b = pl.pallas_call(
        functools.partial(_cast_w2_body, n_valid=nb),
        grid=(fp // FFN_TILE_F,),
        in_specs=w2_specs,
        out_specs=pl.BlockSpec((FFN_TILE_F, d), lambda c: (c, 0)),
        out_shape=jax.ShapeDtypeStruct((fp, d), BF16),
        compiler_params=_params(1),
        name="cast_w2",
    )(*([w2] * per))
    return w1b, w2b


def kernel(x, ffn1_norm, ffn1_w1, ffn1_w2, mix_norm, w_in, conv_a_w, w_out_a, conv_b_w, conv_b_bias,
           ln_b_gain, ln_b_bias, w_out_b, conv_c_w, ig_bias, fg_bias, mh_norm, w_out_c, w_gate, b_gate,
           w_o, ffn2_norm, ffn2_w1, ffn2_w2, final_norm):
    bsz, s, d = x.shape
    depth = ffn1_norm.shape[0]
    d_a, d_b, d_c = conv_a_w.shape[-1], conv_b_w.shape[-1], mh_norm.shape[-1]
    n_heads = ig_bias.shape[-1]
    chunk = _pick(s, MLSTM_CHUNK)
    col_b = 3 * d_a
    col_q = col_b + 2 * d_b
    col_v = col_q + 2 * d_c
    col_o = col_v + d_c
    col_g = col_o + d_c
    vec = lambda a: a.reshape(1, -1).astype(F32)
    qk_scale = jnp.concatenate(
        [jnp.full((1, d_c), (d_c // n_heads) ** -0.5, F32), jnp.ones((1, d_c), F32)], axis=-1)

    side_ok = _ffn_can_side(s, ffn1_w2.shape[1])
    w_in_b = w_in.astype(BF16)
    wa_b, wb_b, wc_b = w_out_a.astype(BF16), w_out_b.astype(BF16), w_out_c.astype(BF16)
    wg_b, wo_b = w_gate.astype(BF16), w_o.astype(BF16)
    bg3 = b_gate.reshape(depth, 1, -1).astype(F32)
    gate_pad = ((0, 0), (0, LANES - 2 * n_heads))

    outs = []
    for b in range(bsz):
        xb = x[b]
        f1w = _ffn_weights(ffn1_w1, ffn1_w2, 0)
        for l in range(depth):
            res = _ffn(xb, vec(ffn1_norm[l]), *f1w, vec(mix_norm[l]), emit_x=True, emit_h=True, h_dtype=BF16,
                       side=(ffn2_w1, ffn2_w2, l) if side_ok else None)
            xb, h = res[0], res[1]
            f2w = res[2:] if side_ok else _ffn_weights(ffn2_w1, ffn2_w2, l)

            pa = _proj_a(h, w_in_b, l, conv_a_w, d_a)
            pb = _proj_b(h, w_in_b, l, conv_b_w, vec(conv_b_bias[l]), vec(ln_b_gain[l]),
                         vec(ln_b_bias[l]), col_b)
            qk = _proj_qk(h, w_in_b, l, col_q, conv_c_w, qk_scale)
            v = _proj_plain(h, w_in_b, l, col_v, d_c, gate=False, name="proj_v")
            o = _proj_plain(h, w_in_b, l, col_o, d_c, gate=True, name="proj_o")
            w_g = jnp.pad(w_in_b[l, :, col_g:], gate_pad)
            b_g = jnp.pad(jnp.concatenate([ig_bias[l], fg_bias[l]]).reshape(1, -1).astype(F32), gate_pad)
            gc, gr = _gates(h, w_g, b_g, n_heads, chunk)
            pc = _mlstm(qk, v, o, gc, gr, vec(mh_norm[l]), n_heads, chunk)

            mix = _mix(pa, pb, pc, h, wa_b, wb_b, wc_b, wg_b, bg3, l)
            xb = _out(xb, mix, wo_b, l)

            last = l == depth - 1
            res = _ffn(xb, vec(ffn2_norm[l]), *f2w, vec(final_norm), emit_x=not last, emit_h=last, h_dtype=F32,
                       side=(ffn1_w1, ffn1_w2, l + 1) if side_ok and not last else None)
            xb = res[0]
            if not last:
                f1w = res[1:] if side_ok else _ffn_weights(ffn1_w1, ffn1_w2, l + 1)
        outs.append(xb)
    return outs[0][None] if bsz == 1 else jnp.stack(outs, axis=0)
```

```python
import functools
import math

import jax
import jax.numpy as jnp
from jax import lax
from jax.experimental import pallas as pl
from jax.experimental.pallas import tpu as pltpu

EPS = 1e-6
F32 = jnp.float32
BF16 = jnp.bfloat16

LANES = 128
SUBLANES = 8
VMEM_LIMIT_BYTES = 56 * 1024 * 1024

MLSTM_CHUNK = 256
EPI_ROW_BLOCK = 64
FFN_TILE_F = 512
MXU_PIECE = 256


def _pick(n, *cands):
    for c in cands:
        if c <= n and n % c == 0:
            return c
    return n


def _params(n_axes):
    return pltpu.CompilerParams(
        dimension_semantics=("arbitrary",) * n_axes, vmem_limit_bytes=VMEM_LIMIT_BYTES)


def _dot(a, b):
    return jnp.dot(a, b, preferred_element_type=F32)


def _rms(x, g):
    return x * lax.rsqrt(jnp.mean(x * x, axis=-1, keepdims=True) + EPS) * g


def _silu(x):
    return x * jax.nn.sigmoid(x)


def _cast_slab(x_ref, o_ref, valid):
    o_ref[...] = jnp.where(valid, x_ref[...], 0.0).astype(o_ref.dtype)


def _ffn_body(x_ref, gg_ref, w1_ref, w2_ref, *refs, emit_x, emit_h, side):
    n_side_in = 2 if side else 0
    side_in, refs = refs[:n_side_in], refs[n_side_in:]
    n_out = int(emit_x) + int(emit_h)
    outs, side_out = list(refs[:n_out]), refs[n_out:n_out + n_side_in]
    h_sc, acc_sc = refs[n_out + n_side_in:]
    f = pl.program_id(1)

    @pl.when(f == 0)
    def _():
        h_sc[...] = _rms(x_ref[...], gg_ref[0:1, :]).astype(BF16)
        acc_sc[...] = jnp.zeros_like(acc_sc)

    h = h_sc[...]
    gate = _dot(h, w1_ref[0])
    up = _dot(h, w1_ref[1])
    p = (_silu(gate) * up).astype(BF16)
    acc_sc[...] += _dot(p, w2_ref[...])

    @pl.when(f == pl.num_programs(1) - 1)
    def _():
        xn = x_ref[...] + 0.5 * acc_sc[...]
        if emit_x:
            outs[0][...] = xn
        if emit_h:
            outs[-1][...] = _rms(xn, gg_ref[1:2, :]).astype(outs[-1].dtype)

    if side:
        n_slab, n_valid = side
        t = pl.program_id(0) * pl.num_programs(1) + f

        @pl.when(t < 2 * n_slab)
        def _():
            _cast_slab(side_in[0], side_out[0], t % n_slab < n_valid)

        @pl.when((t >= 2 * n_slab) & (t < 3 * n_slab))
        def _():
            _cast_slab(side_in[1], side_out[1], t - 2 * n_slab < n_valid)


def _ffn_tiles(s, fp):
    tm = _pick(s, 512, 256, 128)
    tf = _pick(fp, FFN_TILE_F, 256, 128)
    return tm, tf


def _ffn_can_side(s, d_ff):
    fp = -(-d_ff // FFN_TILE_F) * FFN_TILE_F
    tm, tf = _ffn_tiles(s, fp)
    return d_ff % LANES == 0 and (s // tm) * (fp // tf) >= 3 * (fp // LANES)


def _ffn(x, g, w1, w2, gn, *, emit_x, emit_h, h_dtype, side=None):
    s, d = x.shape
    fp = w2.shape[0]
    tm, tf = _ffn_tiles(s, fp)
    nf = fp // tf
    row = lambda i, f: (i, 0)
    in_specs = [
        pl.BlockSpec((tm, d), row),
        pl.BlockSpec((2, d), lambda i, f: (0, 0)),
        pl.BlockSpec((2, d, tf), lambda i, f: (0, 0, f)),
        pl.BlockSpec((tf, d), lambda i, f: (f, 0)),
    ]
    operands = [x, jnp.concatenate([g, gn], axis=0), w1, w2]
    out_shape, out_specs = [], []
    if emit_x:
        out_shape.append(jax.ShapeDtypeStruct((s, d), F32))
        out_specs.append(pl.BlockSpec((tm, d), row))
    if emit_h:
        out_shape.append(jax.ShapeDtypeStruct((s, d), h_dtype))
        out_specs.append(pl.BlockSpec((tm, d), row))
    side_cfg = None
    if side is not None:
        w1f, w2f, layer = side
        d_ff = w2f.shape[1]
        n_slab, n_valid = fp // LANES, d_ff // LANES
        assert _ffn_can_side(s, d_ff)
        side_cfg = (n_slab, n_valid)
        ta = lambda i, f: jnp.minimum(i * nf + f, 2 * n_slab - 1)
        tb = lambda i, f: jnp.clip(i * nf + f - 2 * n_slab, 0, n_slab - 1)
        in_specs += [
            pl.BlockSpec((None, d, LANES), lambda i, f: (
                layer, 0, (ta(i, f) // n_slab) * n_valid + jnp.minimum(ta(i, f) % n_slab, n_valid - 1))),
            pl.BlockSpec((None, LANES, d), lambda i, f: (layer, jnp.minimum(tb(i, f), n_valid - 1), 0)),
        ]
        operands += [w1f, w2f]
        out_shape += [jax.ShapeDtypeStruct((2, d, fp), BF16), jax.ShapeDtypeStruct((fp, d), BF16)]
        out_specs += [
            pl.BlockSpec((None, d, LANES), lambda i, f: (ta(i, f) // n_slab, 0, ta(i, f) % n_slab)),
            pl.BlockSpec((LANES, d), lambda i, f: (tb(i, f), 0)),
        ]
    return pl.pallas_call(
        functools.partial(_ffn_body, emit_x=emit_x, emit_h=emit_h, side=side_cfg),
        grid=(s // tm, nf),
        in_specs=in_specs,
        out_specs=out_specs,
        out_shape=out_shape,
        scratch_shapes=[pltpu.VMEM((tm, d), BF16), pltpu.VMEM((tm, d), F32)],
        compiler_params=_params(2),
        name="ffn",
    )(*operands)


def _proj_a_body(h_ref, wb_ref, wc_ref, wx_ref, cw_ref, o_ref, win_sc, *, halo):
    tm = h_ref.shape[0]
    k_width = cw_ref.shape[0]

    @pl.when(pl.program_id(1) == 0)
    def _():
        win_sc[0:halo, :] = jnp.zeros((halo, win_sc.shape[1]), F32)

    h = h_ref[...]
    win_sc[halo:halo + tm, :] = _dot(h, wc_ref[...]) * _dot(h, wx_ref[...])
    conv = None
    for k in range(k_width):
        term = cw_ref[k:k + 1, :] * win_sc[pl.ds(halo - (k_width - 1 - k), tm), :]
        conv = term if conv is None else conv + term
    o_ref[...] = (_dot(h, wb_ref[...]) * conv).astype(o_ref.dtype)
    win_sc[0:halo, :] = win_sc[tm:tm + halo, :]


def _proj_a(h, w_in, layer, conv_w, d_a):
    s, d = h.shape
    tm = _pick(s, 512, 256, 128)
    tn = _pick(d_a, 512, 256, 128)
    nj = d_a // tn
    halo = SUBLANES
    wspec = lambda off: pl.BlockSpec((None, d, tn), lambda j, i, off=off: (layer, 0, off + j))
    return pl.pallas_call(
        functools.partial(_proj_a_body, halo=halo),
        grid=(nj, s // tm),
        in_specs=[
            pl.BlockSpec((tm, d), lambda j, i: (i, 0)),
            wspec(0), wspec(nj), wspec(2 * nj),
            pl.BlockSpec((None, conv_w.shape[1], tn), lambda j, i: (layer, 0, j)),
        ],
        out_specs=pl.BlockSpec((tm, tn), lambda j, i: (i, j)),
        out_shape=jax.ShapeDtypeStruct((s, d_a), BF16),
        scratch_shapes=[pltpu.VMEM((halo + tm, tn), F32)],
        compiler_params=_params(2),
        name="proj_a",
    )(h, w_in, w_in, w_in, conv_w)


def _halo_reset(win_sc, halo):
    @pl.when(pl.program_id(1) == 0)
    def _():
        win_sc[0:halo, :] = jnp.zeros((halo, win_sc.shape[1]), F32)


def _halo_carry(win_sc, halo, tm):
    win_sc[0:halo, :] = win_sc[tm:tm + halo, :]


def _shift_rows(a, j):
    n = a.shape[0] // SUBLANES
    if j == 0:
        return a[SUBLANES:, :]
    rolled = [pltpu.roll(a[p * SUBLANES:(p + 1) * SUBLANES, :], j, axis=0) for p in range(n)]
    from_prev = lax.broadcasted_iota(jnp.int32, (SUBLANES, a.shape[1]), 0) < j
    return jnp.concatenate([jnp.where(from_prev, rolled[m], rolled[m + 1]) for m in range(n - 1)], axis=0)


def _proj_qk_body(h_ref, w_ref, cw_ref, sc_ref, o_ref, win_sc, *, halo):
    tm, tn = o_ref.shape
    k_width = cw_ref.shape[0]
    rb = _pick(tm, EPI_ROW_BLOCK)
    _halo_reset(win_sc, halo)
    h = h_ref[...]
    pc = _pick(tn, MXU_PIECE, LANES)
    for p0 in range(0, tn, pc):
        ps = slice(p0, p0 + pc)
        win_sc[halo:halo + tm, ps] = _dot(h, w_ref[:, ps])
        for c0 in range(p0, p0 + pc, LANES):
            cs = slice(c0, c0 + LANES)
            for r0 in range(0, tm, rb):
                lo = halo + r0 - SUBLANES
                a = win_sc[lo:lo + rb + SUBLANES, cs]
                y = None
                for j in range(k_width):
                    term = cw_ref[k_width - 1 - j:k_width - j, cs] * _shift_rows(a, j)
                    y = term if y is None else y + term
                o_ref[r0:r0 + rb, cs] = (_silu(y) * sc_ref[:, cs]).astype(o_ref.dtype)
    _halo_carry(win_sc, halo, tm)


def _proj_qk(h, w_in, layer, col0, conv_w, scale):
    s, d = h.shape
    k_width, n = conv_w.shape[1:]
    tm = _pick(s, 512, 256, 128)
    tn = _pick(math.gcd(n, col0), 1024, 512, 256, 128)
    halo = SUBLANES
    j0 = col0 // tn
    return pl.pallas_call(
        functools.partial(_proj_qk_body, halo=halo),
        grid=(n // tn, s // tm),
        in_specs=[
            pl.BlockSpec((tm, d), lambda j, i: (i, 0)),
            pl.BlockSpec((None, d, tn), lambda j, i: (layer, 0, j0 + j)),
            pl.BlockSpec((None, k_width, tn), lambda j, i: (layer, 0, j)),
            pl.BlockSpec((1, tn), lambda j, i: (0, j)),
        ],
        out_specs=pl.BlockSpec((tm, tn), lambda j, i: (i, j)),
        out_shape=jax.ShapeDtypeStruct((s, n), BF16),
        scratch_shapes=[pltpu.VMEM((halo + tm, tn), F32)],
        compiler_params=_params(2),
        name="proj_qk",
    )(h, w_in, conv_w, scale)


def _conv_wide(win_sc, cw_ref, cb_ref, z_sc, halo, tm, c_lo, c_hi):
    k_width = cw_ref.shape[0]
    rb = _pick(tm, 2 * EPI_ROW_BLOCK, EPI_ROW_BLOCK)
    for c0 in range(c_lo, c_hi, LANES):
        cs = slice(c0, c0 + LANES)
        for r0 in range(0, tm, rb):
            z = jnp.zeros((rb, LANES), F32) + cb_ref[:, cs]
            for r in range(min(SUBLANES, k_width)):
                y = None
                for j in range(r, k_width, SUBLANES):
                    lo = halo + r0 - SUBLANES - (j - r)
                    term = cw_ref[k_width - 1 - j:k_width - j, cs] * win_sc[lo:lo + rb + SUBLANES, cs]
                    y = term if y is None else y + term
                z = z + _shift_rows(y, r)
            z_sc[r0:r0 + rb, cs] = z


def _proj_b_body(h_ref, wv_ref, wg_ref, cw_ref, cb_ref, lg_ref, lb_ref, o_ref, win_sc, z_sc, *, halo):
    tm, n = o_ref.shape
    rb = _pick(tm, EPI_ROW_BLOCK)
    _halo_reset(win_sc, halo)
    h = h_ref[...]
    pc = _pick(n, MXU_PIECE, LANES)
    for p0 in range(0, n, pc):
        ps = slice(p0, p0 + pc)
        win_sc[halo:halo + tm, ps] = _dot(h, wv_ref[:, ps]) * jax.nn.sigmoid(_dot(h, wg_ref[:, ps]))
        _conv_wide(win_sc, cw_ref, cb_ref, z_sc, halo, tm, p0, p0 + pc)
    for r0 in range(0, tm, rb):
        z = z_sc[r0:r0 + rb, :]
        zc = z - jnp.mean(z, axis=-1, keepdims=True)
        y = zc * lax.rsqrt(jnp.mean(zc * zc, axis=-1, keepdims=True) + EPS) * lg_ref[...] + lb_ref[...]
        o_ref[r0:r0 + rb, :] = _silu(y).astype(o_ref.dtype)
    _halo_carry(win_sc, halo, tm)


def _proj_b(h, w_in, layer, conv_w, conv_b, ln_g, ln_b, col0):
    s, d = h.shape
    k_width, d_b = conv_w.shape[1:]
    tm = _pick(s, 256, 128)
    halo = SUBLANES * (1 + (k_width - 1) // SUBLANES)
    assert col0 % d_b == 0
    jb = col0 // d_b
    vec = pl.BlockSpec((1, d_b), lambda j, i: (0, 0))
    return pl.pallas_call(
        functools.partial(_proj_b_body, halo=halo),
        grid=(1, s // tm),
        in_specs=[
            pl.BlockSpec((tm, d), lambda j, i: (i, 0)),
            pl.BlockSpec((None, d, d_b), lambda j, i: (layer, 0, jb)),
            pl.BlockSpec((None, d, d_b), lambda j, i: (layer, 0, jb + 1)),
            pl.BlockSpec((None, k_width, d_b), lambda j, i: (layer, 0, 0)),
            vec, vec, vec,
        ],
        out_specs=pl.BlockSpec((tm, d_b), lambda j, i: (i, 0)),
        out_shape=jax.ShapeDtypeStruct((s, d_b), BF16),
        scratch_shapes=[pltpu.VMEM((halo + tm, d_b), F32), pltpu.VMEM((tm, d_b), F32)],
        compiler_params=_params(2),
        name="proj_b",
    )(h, w_in, w_in, conv_w, conv_b, ln_g, ln_b)


def _proj_plain_body(h_ref, w_ref, o_ref, *, gate):
    h = h_ref[...]
    pc = _pick(o_ref.shape[1], MXU_PIECE, LANES)
    for p0 in range(0, o_ref.shape[1], pc):
        ps = slice(p0, p0 + pc)
        y = _dot(h, w_ref[:, ps])
        if gate:
            y = jax.nn.sigmoid(y)
        o_ref[:, ps] = y.astype(o_ref.dtype)


def _proj_plain(h, w_in, layer, col0, n, *, gate, name):
    s, d = h.shape
    tm = _pick(s, 512, 256, 128)
    tn = _pick(math.gcd(n, col0), 1024, 512, 256, 128)
    j0 = col0 // tn
    return pl.pallas_call(
        functools.partial(_proj_plain_body, gate=gate),
        grid=(n // tn, s // tm),
        in_specs=[
            pl.BlockSpec((tm, d), lambda j, i: (i, 0)),
            pl.BlockSpec((None, d, tn), lambda j, i: (layer, 0, j0 + j)),
        ],
        out_specs=pl.BlockSpec((tm, tn), lambda j, i: (i, j)),
        out_shape=jax.ShapeDtypeStruct((s, n), BF16),
        compiler_params=_params(2),
        name=name,
    )(h, w_in)


def _gates_body(h_ref, w_ref, b_ref, gc_ref, gr_ref, *, n_heads, chunk):
    r = lax.broadcasted_iota(jnp.int32, (chunk, chunk), 0)
    c = lax.broadcasted_iota(jnp.int32, (chunk, chunk), 1)
    tril = (c <= r).astype(F32)
    lane = lax.broadcasted_iota(jnp.int32, (chunk, LANES), 1)
    for t0 in range(0, h_ref.shape[0], chunk):
        g = _dot(h_ref[t0:t0 + chunk, :], w_ref[...]) + b_ref[...]
        lf = jnp.minimum(g, 0.0) - jnp.log1p(jnp.exp(-jnp.abs(g)))
        cum = jnp.dot(tril, lf, precision=lax.Precision.HIGHEST, preferred_element_type=F32)
        out = jnp.where(lane < n_heads, g, jnp.where(lane < 2 * n_heads, cum, 0.0))
        gc_ref[t0:t0 + chunk, :] = out
        gr_ref[:, t0:t0 + chunk] = out.T[0:gr_ref.shape[0], :]


def _gates(h, w_g, b_g, n_heads, chunk):
    s, d = h.shape
    assert 2 * n_heads <= SUBLANES
    tm = _pick(s, 4 * chunk, 2 * chunk, chunk)
    return pl.pallas_call(
        functools.partial(_gates_body, n_heads=n_heads, chunk=chunk),
        grid=(s // tm,),
        in_specs=[
            pl.BlockSpec((tm, d), lambda c: (c, 0)),
            pl.BlockSpec((d, LANES), lambda c: (0, 0)),
            pl.BlockSpec((1, LANES), lambda c: (0, 0)),
        ],
        out_specs=[
            pl.BlockSpec((tm, LANES), lambda c: (c, 0)),
            pl.BlockSpec((SUBLANES, tm), lambda c: (0, c)),
        ],
        out_shape=[jax.ShapeDtypeStruct((s, LANES), F32), jax.ShapeDtypeStruct((SUBLANES, s), F32)],
        compiler_params=_params(1),
        name="gates",
    )(h, w_g, b_g)


def _mlstm_body(q_ref, k_ref, v_ref, o_ref, gc_ref, gr_ref, mh_ref, out_ref, c_sc, n_sc, m_sc,
                *, n_heads):
    chunk = q_ref.shape[0]
    dh = q_ref.shape[1] // n_heads

    @pl.when(pl.program_id(0) == 0)
    def _():
        c_sc[...] = jnp.zeros_like(c_sc)
        n_sc[...] = jnp.zeros_like(n_sc)
        m_sc[...] = jnp.zeros_like(m_sc)

    r = lax.broadcasted_iota(jnp.int32, (chunk, chunk), 0)
    c = lax.broadcasted_iota(jnp.int32, (chunk, chunk), 1)
    causal = c <= r
    gc = gc_ref[...]
    gr = gr_ref[...]
    for hd in range(n_heads):
        sl = slice(hd * dh, (hd + 1) * dh)
        q, k, v = q_ref[:, sl], k_ref[:, sl], v_ref[:, sl]
        ig_c, b_c = gc[:, hd:hd + 1], gc[:, n_heads + hd:n_heads + hd + 1]
        ig_r, b_r = gr[hd:hd + 1, :], gr[n_heads + hd:n_heads + hd + 1, :]
        m_prev = m_sc[hd * SUBLANES:hd * SUBLANES + 1, 0:1]
        c_prev = c_sc[hd]
        n_prev = n_sc[hd * SUBLANES:hd * SUBLANES + 1, :]

        inter = b_c + m_prev
        dmat = jnp.where(causal, b_c - b_r + ig_r, -jnp.inf)
        m_t = jnp.maximum(inter, jnp.max(dmat, axis=-1, keepdims=True))
        w_inter = jnp.exp(inter - m_t)
        s = lax.dot_general(q, k, (((1,), (1,)), ((), ())), preferred_element_type=F32)
        s = s * jnp.exp(dmat - m_t)
        num = w_inter * _dot(q, c_prev.astype(BF16)) + _dot(s.astype(BF16), v)
        qn = jnp.sum(q.astype(F32) * n_prev, axis=-1, keepdims=True)
        den = w_inter * qn + jnp.sum(s, axis=-1, keepdims=True)
        ht = num / jnp.maximum(jnp.abs(den), jnp.exp(-m_t))
        ht = _rms(ht, mh_ref[:, sl])
        out_ref[:, sl] = (o_ref[:, sl].astype(F32) * ht).astype(out_ref.dtype)

        m_new = m_t[chunk - 1:chunk, :]
        b_last = b_c[chunk - 1:chunk, :]
        decay = jnp.exp(b_last + m_prev - m_new)
        kw = k.astype(F32) * jnp.exp(b_last - b_c + ig_c - m_new)
        c_sc[hd] = decay * c_prev + lax.dot_general(
            kw.astype(BF16), v, (((0,), (0,)), ((), ())), preferred_element_type=F32)
        n_sc[hd * SUBLANES:(hd + 1) * SUBLANES, :] = jnp.broadcast_to(
            decay * n_prev + jnp.sum(kw, axis=0, keepdims=True), (SUBLANES, dh))
        m_sc[hd * SUBLANES:(hd + 1) * SUBLANES, :] = jnp.broadcast_to(m_new, (SUBLANES, LANES))


def _mlstm(qk, v, o, gc, gr, mh, n_heads, chunk):
    s, d_c = v.shape
    dh = d_c // n_heads
    blk = pl.BlockSpec((chunk, d_c), lambda c: (c, 0))
    return pl.pallas_call(
        functools.partial(_mlstm_body, n_heads=n_heads),
        grid=(s // chunk,),
        in_specs=[
            blk,
            pl.BlockSpec((chunk, d_c), lambda c: (c, 1)),
            blk, blk,
            pl.BlockSpec((chunk, LANES), lambda c: (c, 0)),
            pl.BlockSpec((SUBLANES, chunk), lambda c: (0, c)),
            pl.BlockSpec((1, d_c), lambda c: (0, 0)),
        ],
        out_specs=blk,
        out_shape=jax.ShapeDtypeStruct((s, d_c), BF16),
        scratch_shapes=[
            pltpu.VMEM((n_heads, dh, dh), F32),
            pltpu.VMEM((n_heads * SUBLANES, dh), F32),
            pltpu.VMEM((n_heads * SUBLANES, LANES), F32),
        ],
        compiler_params=_params(1),
        name="mlstm",
    )(qk, qk, v, o, gc, gr, mh)


def _mix_body(pa_ref, pb_ref, pc_ref, h_ref, wa_ref, wb_ref, wc_ref, wga_ref, wgb_ref, wgc_ref,
              b_ref, o_ref):
    h = h_ref[...]
    acc = jax.nn.sigmoid(_dot(h, wga_ref[...]) + b_ref[0:1, :]) * _dot(pa_ref[...], wa_ref[...])
    acc += jax.nn.sigmoid(_dot(h, wgb_ref[...]) + b_ref[1:2, :]) * _dot(pb_ref[...], wb_ref[...])
    acc += jax.nn.sigmoid(_dot(h, wgc_ref[...]) + b_ref[2:3, :]) * _dot(pc_ref[...], wc_ref[...])
    o_ref[...] = acc.astype(o_ref.dtype)


def _mix(pa, pb, pc, h, wa, wb, wc, wg, bg, layer):
    s, d = h.shape
    tm = _pick(s, 512, 256, 128)
    tn = _pick(d, 512, 256, 128)
    nj = d // tn
    rows = lambda n: pl.BlockSpec((tm, n), lambda j, i: (i, 0))
    cols = lambda kdim, off: pl.BlockSpec((None, kdim, tn), lambda j, i, off=off: (layer, 0, off + j))
    return pl.pallas_call(
        _mix_body,
        grid=(nj, s // tm),
        in_specs=[
            rows(pa.shape[1]), rows(pb.shape[1]), rows(pc.shape[1]), rows(d),
            cols(wa.shape[1], 0), cols(wb.shape[1], 0), cols(wc.shape[1], 0),
            cols(d, 0), cols(d, nj), cols(d, 2 * nj),
            cols(bg.shape[1], 0),
        ],
        out_specs=pl.BlockSpec((tm, tn), lambda j, i: (i, j)),
        out_shape=jax.ShapeDtypeStruct((s, d), BF16),
        compiler_params=_params(2),
        name="mix",
    )(pa, pb, pc, h, wa, wb, wc, wg, wg, wg, bg)


def _out_body(x_ref, m_ref, w_ref, o_ref):
    o_ref[...] = x_ref[...] + _dot(m_ref[...], w_ref[...])


def _out(x, mix, w_o, layer):
    s, d = x.shape
    tm = _pick(s, 512, 256, 128)
    row = pl.BlockSpec((tm, d), lambda i: (i, 0))
    return pl.pallas_call(
        _out_body,
        grid=(s // tm,),
        in_specs=[row, row, pl.BlockSpec((None, d, d), lambda i: (layer, 0, 0))],
        out_specs=row,
        out_shape=jax.ShapeDtypeStruct((s, d), F32),
        compiler_params=_params(1),
        name="out_proj",
    )(x, mix, w_o)


def _cast_w1_body(*refs, n_valid):
    ins, o_ref = refs[:-1], refs[-1]
    c = pl.program_id(1)
    for q, x_ref in enumerate(ins):
        valid = c * len(ins) + q < n_valid
        o_ref[:, q * LANES:(q + 1) * LANES] = jnp.where(valid, x_ref[...], 0.0).astype(o_ref.dtype)


def _cast_w2_body(*refs, n_valid):
    ins, o_ref = refs[:-1], refs[-1]
    c = pl.program_id(0)
    for q, x_ref in enumerate(ins):
        valid = c * len(ins) + q < n_valid
        o_ref[q * LANES:(q + 1) * LANES, :] = jnp.where(valid, x_ref[...], 0.0).astype(o_ref.dtype)


def _ffn_weights(w1, w2, layer):
    _, d, _ = w1.shape
    d_ff = w2.shape[1]
    assert d_ff % LANES == 0
    nb = d_ff // LANES
    per = FFN_TILE_F // LANES
    fp = -(-d_ff // FFN_TILE_F) * FFN_TILE_F
    w1_specs = [pl.BlockSpec((None, d, LANES),
                             lambda g, c, q=q: (layer, 0, g * nb + jnp.minimum(c * per + q, nb - 1)))
                for q in range(per)]
    w1b = pl.pallas_call(
        functools.partial(_cast_w1_body, n_valid=nb),
        grid=(2, fp // FFN_TILE_F),
        in_specs=w1_specs,
        out_specs=pl.BlockSpec((None, d, FFN_TILE_F), lambda g, c: (g, 0, c)),
        out_shape=jax.ShapeDtypeStruct((2, d, fp), BF16),
        compiler_params=_params(2),
        name="cast_w1",
    )(*([w1] * per))
    w2_specs = [pl.BlockSpec((None, LANES, d), lambda c, q=q: (layer, jnp.minimum(c * per + q, nb - 1), 0))
                for q in range(per)]
    w2b = pl.pallas_call(
        functools.partial(_cast_w2_body, n_valid=nb),
        grid=(fp // FFN_TILE_F,),
        in_specs=w2_specs,
        out_specs=pl.BlockSpec((FFN_TILE_F, d), lambda c: (c, 0)),
        out_shape=jax.ShapeDtypeStruct((fp, d), BF16),
        compiler_params=_params(1),
        name="cast_w2",
    )(*([w2] * per))
    return w1b, w2b


def kernel(x, ffn1_norm, ffn1_w1, ffn1_w2, mix_norm, w_in, conv_a_w, w_out_a, conv_b_w, conv_b_bias,
           ln_b_gain, ln_b_bias, w_out_b, conv_c_w, ig_bias, fg_bias, mh_norm, w_out_c, w_gate, b_gate,
           w_o, ffn2_norm, ffn2_w1, ffn2_w2, final_norm):
    bsz, s, d = x.shape
    depth = ffn1_norm.shape[0]
    d_a, d_b, d_c = conv_a_w.shape[-1], conv_b_w.shape[-1], mh_norm.shape[-1]
    n_heads = ig_bias.shape[-1]
    chunk = _pick(s, MLSTM_CHUNK)
    col_b = 3 * d_a
    col_q = col_b + 2 * d_b
    col_v = col_q + 2 * d_c
    col_o = col_v + d_c
    col_g = col_o + d_c
    vec = lambda a: a.reshape(1, -1).astype(F32)
    qk_scale = jnp.concatenate(
        [jnp.full((1, d_c), (d_c // n_heads) ** -0.5, F32), jnp.ones((1, d_c), F32)], axis=-1)

    side_ok = _ffn_can_side(s, ffn1_w2.shape[1])
    w_in_b = w_in.astype(BF16)
    wa_b, wb_b, wc_b = w_out_a.astype(BF16), w_out_b.astype(BF16), w_out_c.astype(BF16)
    wg_b, wo_b = w_gate.astype(BF16), w_o.astype(BF16)
    bg3 = b_gate.reshape(depth, 3, d).astype(F32)
    gate_pad = ((0, 0), (0, LANES - 2 * n_heads))

    outs = []
    for b in range(bsz):
        xb = x[b]
        f1w = _ffn_weights(ffn1_w1, ffn1_w2, 0)
        for l in range(depth):
            res = _ffn(xb, vec(ffn1_norm[l]), *f1w, vec(mix_norm[l]), emit_x=True, emit_h=True, h_dtype=BF16,
                       side=(ffn2_w1, ffn2_w2, l) if side_ok else None)
            xb, h = res[0], res[1]
            f2w = res[2:] if side_ok else _ffn_weights(ffn2_w1, ffn2_w2, l)

            pa = _proj_a(h, w_in_b, l, conv_a_w, d_a)
            pb = _proj_b(h, w_in_b, l, conv_b_w, vec(conv_b_bias[l]), vec(ln_b_gain[l]),
                         vec(ln_b_bias[l]), col_b)
            qk = _proj_qk(h, w_in_b, l, col_q, conv_c_w, qk_scale)
            v = _proj_plain(h, w_in_b, l, col_v, d_c, gate=False, name="proj_v")
            o = _proj_plain(h, w_in_b, l, col_o, d_c, gate=True, name="proj_o")
            w_g = jnp.pad(w_in_b[l, :, col_g:], gate_pad)
            b_g = jnp.pad(jnp.concatenate([ig_bias[l], fg_bias[l]]).reshape(1, -1).astype(F32), gate_pad)
            gc, gr = _gates(h, w_g, b_g, n_heads, chunk)
            pc = _mlstm(qk, v, o, gc, gr, vec(mh_norm[l]), n_heads, chunk)

            mix = _mix(pa, pb, pc, h, wa_b, wb_b, wc_b, wg_b, bg3, l)
            xb = _out(xb, mix, wo_b, l)

            last = l == depth - 1
            res = _ffn(xb, vec(ffn2_norm[l]), *f2w, vec(final_norm), emit_x=not last, emit_h=last, h_dtype=F32,
                       side=(ffn1_w1, ffn1_w2, l + 1) if side_ok and not last else None)
            xb = res[0]
            if not last:
                f1w = res[1:] if side_ok else _ffn_weights(ffn1_w1, ffn1_w2, l + 1)
        outs.append(xb)
    return outs[0][None] if bsz == 1 else jnp.stack(outs, axis=0)
```

```python
import functools
import math

import jax
import jax.numpy as jnp
from jax import lax
from jax.experimental import pallas as pl
from jax.experimental.pallas import tpu as pltpu

EPS = 1e-6
F32 = jnp.float32
BF16 = jnp.bfloat16

LANES = 128
SUBLANES = 8
VMEM_LIMIT_BYTES = 56 * 1024 * 1024

MLSTM_CHUNK = 256
EPI_ROW_BLOCK = 64
FFN_TILE_F = 512
MXU_PIECE = 256


def _pick(n, *cands):
    for c in cands:
        if c <= n and n % c == 0:
            return c
    return n


def _params(n_axes):
    return pltpu.CompilerParams(
        dimension_semantics=("arbitrary",) * n_axes, vmem_limit_bytes=VMEM_LIMIT_BYTES)


def _dot(a, b):
    return jnp.dot(a, b, preferred_element_type=F32)


def _rms(x, g):
    return x * lax.rsqrt(jnp.mean(x * x, axis=-1, keepdims=True) + EPS) * g


def _silu(x):
    return x * jax.nn.sigmoid(x)


def _cast_slab(x_ref, o_ref, valid):
    o_ref[...] = jnp.where(valid, x_ref[...], 0.0).astype(o_ref.dtype)


def _ffn_body(x_ref, gg_ref, w1_ref, w2_ref, *refs, emit_x, emit_h, side):
    n_side_in = 2 if side else 0
    side_in, refs = refs[:n_side_in], refs[n_side_in:]
    n_out = int(emit_x) + int(emit_h)
    outs, side_out = list(refs[:n_out]), refs[n_out:n_out + n_side_in]
    h_sc, acc_sc = refs[n_out + n_side_in:]
    f = pl.program_id(1)

    @pl.when(f == 0)
    def _():
        h_sc[...] = _rms(x_ref[...], gg_ref[0:1, :]).astype(BF16)
        acc_sc[...] = jnp.zeros_like(acc_sc)

    h = h_sc[...]
    gate = _dot(h, w1_ref[0])
    up = _dot(h, w1_ref[1])
    p = (_silu(gate) * up).astype(BF16)
    acc_sc[...] += _dot(p, w2_ref[...])

    @pl.when(f == pl.num_programs(1) - 1)
    def _():
        xn = x_ref[...] + 0.5 * acc_sc[...]
        if emit_x:
            outs[0][...] = xn
        if emit_h:
            outs[-1][...] = _rms(xn, gg_ref[1:2, :]).astype(outs[-1].dtype)

    if side:
        n_slab, n_valid = side
        t = pl.program_id(0) * pl.num_programs(1) + f

        @pl.when(t < 2 * n_slab)
        def _():
            _cast_slab(side_in[0], side_out[0], t % n_slab < n_valid)

        @pl.when((t >= 2 * n_slab) & (t < 3 * n_slab))
        def _():
            _cast_slab(side_in[1], side_out[1], t - 2 * n_slab < n_valid)


def _ffn_tiles(s, fp):
    tm = _pick(s, 512, 256, 128)
    tf = _pick(fp, FFN_TILE_F, 256, 128)
    return tm, tf


def _ffn_can_side(s, d_ff):
    fp = -(-d_ff // FFN_TILE_F) * FFN_TILE_F
    tm, tf = _ffn_tiles(s, fp)
    return d_ff % LANES == 0 and (s // tm) * (fp // tf) >= 3 * (fp // LANES)


def _ffn(x, g, w1, w2, gn, *, emit_x, emit_h, h_dtype, side=None):
    s, d = x.shape
    fp = w2.shape[0]
    tm, tf = _ffn_tiles(s, fp)
    nf = fp // tf
    row = lambda i, f: (i, 0)
    in_specs = [
        pl.BlockSpec((tm, d), row),
        pl.BlockSpec((2, d), lambda i, f: (0, 0)),
        pl.BlockSpec((2, d, tf), lambda i, f: (0, 0, f)),
        pl.BlockSpec((tf, d), lambda i, f: (f, 0)),
    ]
    operands = [x, jnp.concatenate([g, gn], axis=0), w1, w2]
    out_shape, out_specs = [], []
    if emit_x:
        out_shape.append(jax.ShapeDtypeStruct((s, d), F32))
        out_specs.append(pl.BlockSpec((tm, d), row))
    if emit_h:
        out_shape.append(jax.ShapeDtypeStruct((s, d), h_dtype))
        out_specs.append(pl.BlockSpec((tm, d), row))
    side_cfg = None
    if side is not None:
        w1f, w2f, layer = side
        d_ff = w2f.shape[1]
        n_slab, n_valid = fp // LANES, d_ff // LANES
        assert _ffn_can_side(s, d_ff)
        side_cfg = (n_slab, n_valid)
        ta = lambda i, f: jnp.minimum(i * nf + f, 2 * n_slab - 1)
        tb = lambda i, f: jnp.clip(i * nf + f - 2 * n_slab, 0, n_slab - 1)
        in_specs += [
            pl.BlockSpec((None, d, LANES), lambda i, f: (
                layer, 0, (ta(i, f) // n_slab) * n_valid + jnp.minimum(ta(i, f) % n_slab, n_valid - 1))),
            pl.BlockSpec((None, LANES, d), lambda i, f: (layer, jnp.minimum(tb(i, f), n_valid - 1), 0)),
        ]
        operands += [w1f, w2f]
        out_shape += [jax.ShapeDtypeStruct((2, d, fp), BF16), jax.ShapeDtypeStruct((fp, d), BF16)]
        out_specs += [
            pl.BlockSpec((None, d, LANES), lambda i, f: (ta(i, f) // n_slab, 0, ta(i, f) % n_slab)),
            pl.BlockSpec((LANES, d), lambda i, f: (tb(i, f), 0)),
        ]
    return pl.pallas_call(
        functools.partial(_ffn_body, emit_x=emit_x, emit_h=emit_h, side=side_cfg),
        grid=(s // tm, nf),
        in_specs=in_specs,
        out_specs=out_specs,
        out_shape=out_shape,
        scratch_shapes=[pltpu.VMEM((tm, d), BF16), pltpu.VMEM((tm, d), F32)],
        compiler_params=_params(2),
        name="ffn",
    )(*operands)


def _proj_a_body(h_ref, wb_ref, wc_ref, wx_ref, cw_ref, o_ref, win_sc, *, halo):
    tm = h_ref.shape[0]
    k_width = cw_ref.shape[0]

    @pl.when(pl.program_id(1) == 0)
    def _():
        win_sc[0:halo, :] = jnp.zeros((halo, win_sc.shape[1]), F32)

    h = h_ref[...]
    win_sc[halo:halo + tm, :] = _dot(h, wc_ref[...]) * _dot(h, wx_ref[...])
    conv = None
    for k in range(k_width):
        term = cw_ref[k:k + 1, :] * win_sc[pl.ds(halo - (k_width - 1 - k), tm), :]
        conv = term if conv is None else conv + term
    o_ref[...] = (_dot(h, wb_ref[...]) * conv).astype(o_ref.dtype)
    win_sc[0:halo, :] = win_sc[tm:tm + halo, :]


def _proj_a(h, w_in, layer, conv_w, d_a):
    s, d = h.shape
    tm = _pick(s, 1024, 512, 256, 128)
    tn = _pick(d_a, 512, 256, 128)
    nj = d_a // tn
    halo = SUBLANES
    wspec = lambda off: pl.BlockSpec((None, d, tn), lambda j, i, off=off: (layer, 0, off + j))
    return pl.pallas_call(
        functools.partial(_proj_a_body, halo=halo),
        grid=(nj, s // tm),
        in_specs=[
            pl.BlockSpec((tm, d), lambda j, i: (i, 0)),
            wspec(0), wspec(nj), wspec(2 * nj),
            pl.BlockSpec((None, conv_w.shape[1], tn), lambda j, i: (layer, 0, j)),
        ],
        out_specs=pl.BlockSpec((tm, tn), lambda j, i: (i, j)),
        out_shape=jax.ShapeDtypeStruct((s, d_a), BF16),
        scratch_shapes=[pltpu.VMEM((halo + tm, tn), F32)],
        compiler_params=_params(2),
        name="proj_a",
    )(h, w_in, w_in, w_in, conv_w)


def _halo_reset(win_sc, halo):
    @pl.when(pl.program_id(1) == 0)
    def _():
        win_sc[0:halo, :] = jnp.zeros((halo, win_sc.shape[1]), F32)


def _halo_carry(win_sc, halo, tm):
    win_sc[0:halo, :] = win_sc[tm:tm + halo, :]


def _shift_rows(a, j):
    n = a.shape[0] // SUBLANES
    if j == 0:
        return a[SUBLANES:, :]
    rolled = [pltpu.roll(a[p * SUBLANES:(p + 1) * SUBLANES, :], j, axis=0) for p in range(n)]
    from_prev = lax.broadcasted_iota(jnp.int32, (SUBLANES, a.shape[1]), 0) < j
    return jnp.concatenate([jnp.where(from_prev, rolled[m], rolled[m + 1]) for m in range(n - 1)], axis=0)


def _proj_qk_body(h_ref, w_ref, cw_ref, sc_ref, o_ref, win_sc, *, halo):
    tm, tn = o_ref.shape
    k_width = cw_ref.shape[0]
    rb = _pick(tm, EPI_ROW_BLOCK)
    _halo_reset(win_sc, halo)
    h = h_ref[...]
    pc = _pick(tn, MXU_PIECE, LANES)
    for p0 in range(0, tn, pc):
        ps = slice(p0, p0 + pc)
        win_sc[halo:halo + tm, ps] = _dot(h, w_ref[:, ps])
        for c0 in range(p0, p0 + pc, LANES):
            cs = slice(c0, c0 + LANES)
            for r0 in range(0, tm, rb):
                lo = halo + r0 - SUBLANES
                a = win_sc[lo:lo + rb + SUBLANES, cs]
                y = None
                for j in range(k_width):
                    term = cw_ref[k_width - 1 - j:k_width - j, cs] * _shift_rows(a, j)
                    y = term if y is None else y + term
                o_ref[r0:r0 + rb, cs] = (_silu(y) * sc_ref[:, cs]).astype(o_ref.dtype)
    _halo_carry(win_sc, halo, tm)


def _proj_qk(h, w_in, layer, col0, conv_w, scale):
    s, d = h.shape
    k_width, n = conv_w.shape[1:]
    tm = _pick(s, 1024, 512, 256, 128)
    tn = _pick(math.gcd(n, col0), 1024, 512, 256, 128)
    halo = SUBLANES
    j0 = col0 // tn
    return pl.pallas_call(
        functools.partial(_proj_qk_body, halo=halo),
        grid=(n // tn, s // tm),
        in_specs=[
            pl.BlockSpec((tm, d), lambda j, i: (i, 0)),
            pl.BlockSpec((None, d, tn), lambda j, i: (layer, 0, j0 + j)),
            pl.BlockSpec((None, k_width, tn), lambda j, i: (layer, 0, j)),
            pl.BlockSpec((1, tn), lambda j, i: (0, j)),
        ],
        out_specs=pl.BlockSpec((tm, tn), lambda j, i: (i, j)),
        out_shape=jax.ShapeDtypeStruct((s, n), BF16),
        scratch_shapes=[pltpu.VMEM((halo + tm, tn), F32)],
        compiler_params=_params(2),
        name="proj_qk",
    )(h, w_in, conv_w, scale)


def _conv_wide(win_sc, cw_ref, cb_ref, z_sc, halo, tm, c_lo, c_hi):
    k_width = cw_ref.shape[0]
    rb = _pick(tm, 2 * EPI_ROW_BLOCK, EPI_ROW_BLOCK)
    for c0 in range(c_lo, c_hi, LANES):
        cs = slice(c0, c0 + LANES)
        for r0 in range(0, tm, rb):
            z = jnp.zeros((rb, LANES), F32) + cb_ref[:, cs]
            for r in range(min(SUBLANES, k_width)):
                y = None
                for j in range(r, k_width, SUBLANES):
                    lo = halo + r0 - SUBLANES - (j - r)
                    term = cw_ref[k_width - 1 - j:k_width - j, cs] * win_sc[lo:lo + rb + SUBLANES, cs]
                    y = term if y is None else y + term
                z = z + _shift_rows(y, r)
            z_sc[r0:r0 + rb, cs] = z


def _proj_b_body(h_ref, wv_ref, wg_ref, cw_ref, cb_ref, lg_ref, lb_ref, o_ref, win_sc, z_sc, *, halo):
    tm, n = o_ref.shape
    rb = _pick(tm, EPI_ROW_BLOCK)
    _halo_reset(win_sc, halo)
    h = h_ref[...]
    pc = _pick(n, MXU_PIECE, LANES)
    for p0 in range(0, n, pc):
        ps = slice(p0, p0 + pc)
        win_sc[halo:halo + tm, ps] = _dot(h, wv_ref[:, ps]) * jax.nn.sigmoid(_dot(h, wg_ref[:, ps]))
        _conv_wide(win_sc, cw_ref, cb_ref, z_sc, halo, tm, p0, p0 + pc)
    for r0 in range(0, tm, rb):
        z = z_sc[r0:r0 + rb, :]
        zc = z - jnp.mean(z, axis=-1, keepdims=True)
        y = zc * lax.rsqrt(jnp.mean(zc * zc, axis=-1, keepdims=True) + EPS) * lg_ref[...] + lb_ref[...]
        o_ref[r0:r0 + rb, :] = _silu(y).astype(o_ref.dtype)
    _halo_carry(win_sc, halo, tm)


def _proj_b(h, w_in, layer, conv_w, conv_b, ln_g, ln_b, col0):
    s, d = h.shape
    k_width, d_b = conv_w.shape[1:]
    tm = _pick(s, 256, 128)
    halo = SUBLANES * (1 + (k_width - 1) // SUBLANES)
    assert col0 % d_b == 0
    jb = col0 // d_b
    vec = pl.BlockSpec((1, d_b), lambda j, i: (0, 0))
    return pl.pallas_call(
        functools.partial(_proj_b_body, halo=halo),
        grid=(1, s // tm),
        in_specs=[
            pl.BlockSpec((tm, d), lambda j, i: (i, 0)),
            pl.BlockSpec((None, d, d_b), lambda j, i: (layer, 0, jb)),
            pl.BlockSpec((None, d, d_b), lambda j, i: (layer, 0, jb + 1)),
            pl.BlockSpec((None, k_width, d_b), lambda j, i: (layer, 0, 0)),
            vec, vec, vec,
        ],
        out_specs=pl.BlockSpec((tm, d_b), lambda j, i: (i, 0)),
        out_shape=jax.ShapeDtypeStruct((s, d_b), BF16),
        scratch_shapes=[pltpu.VMEM((halo + tm, d_b), F32), pltpu.VMEM((tm, d_b), F32)],
        compiler_params=_params(2),
        name="proj_b",
    )(h, w_in, w_in, conv_w, conv_b, ln_g, ln_b)


def _proj_plain_body(h_ref, w_ref, o_ref, *, gate):
    h = h_ref[...]
    pc = _pick(o_ref.shape[1], MXU_PIECE, LANES)
    for p0 in range(0, o_ref.shape[1], pc):
        ps = slice(p0, p0 + pc)
        y = _dot(h, w_ref[:, ps])
        if gate:
            y = jax.nn.sigmoid(y)
        o_ref[:, ps] = y.astype(o_ref.dtype)


def _proj_plain(h, w_in, layer, col0, n, *, gate, name):
    s, d = h.shape
    tm = _pick(s, 1024, 512, 256, 128)
    tn = _pick(math.gcd(n, col0), 1024, 512, 256, 128)
    j0 = col0 // tn
    return pl.pallas_call(
        functools.partial(_proj_plain_body, gate=gate),
        grid=(n // tn, s // tm),
        in_specs=[
            pl.BlockSpec((tm, d), lambda j, i: (i, 0)),
            pl.BlockSpec((None, d, tn), lambda j, i: (layer, 0, j0 + j)),
        ],
        out_specs=pl.BlockSpec((tm, tn), lambda j, i: (i, j)),
        out_shape=jax.ShapeDtypeStruct((s, n), BF16),
        compiler_params=_params(2),
        name=name,
    )(h, w_in)


def _gates_body(h_ref, w_ref, b_ref, gc_ref, gr_ref, *, n_heads, chunk):
    r = lax.broadcasted_iota(jnp.int32, (chunk, chunk), 0)
    c = lax.broadcasted_iota(jnp.int32, (chunk, chunk), 1)
    tril = (c <= r).astype(F32)
    lane = lax.broadcasted_iota(jnp.int32, (chunk, LANES), 1)
    for t0 in range(0, h_ref.shape[0], chunk):
        g = _dot(h_ref[t0:t0 + chunk, :], w_ref[...]) + b_ref[...]
        lf = jnp.minimum(g, 0.0) - jnp.log1p(jnp.exp(-jnp.abs(g)))
        cum = jnp.dot(tril, lf, precision=lax.Precision.HIGHEST, preferred_element_type=F32)
        out = jnp.where(lane < n_heads, g, jnp.where(lane < 2 * n_heads, cum, 0.0))
        gc_ref[t0:t0 + chunk, :] = out
        gr_ref[:, t0:t0 + chunk] = out.T[0:gr_ref.shape[0], :]


def _gates(h, w_g, b_g, n_heads, chunk):
    s, d = h.shape
    assert 2 * n_heads <= SUBLANES
    tm = _pick(s, 4 * chunk, 2 * chunk, chunk)
    return pl.pallas_call(
        functools.partial(_gates_body, n_heads=n_heads, chunk=chunk),
        grid=(s // tm,),
        in_specs=[
            pl.BlockSpec((tm, d), lambda c: (c, 0)),
            pl.BlockSpec((d, LANES), lambda c: (0, 0)),
            pl.BlockSpec((1, LANES), lambda c: (0, 0)),
        ],
        out_specs=[
            pl.BlockSpec((tm, LANES), lambda c: (c, 0)),
            pl.BlockSpec((SUBLANES, tm), lambda c: (0, c)),
        ],
        out_shape=[jax.ShapeDtypeStruct((s, LANES), F32), jax.ShapeDtypeStruct((SUBLANES, s), F32)],
        compiler_params=_params(1),
        name="gates",
    )(h, w_g, b_g)


def _mlstm_body(q_ref, k_ref, v_ref, o_ref, gc_ref, gr_ref, mh_ref, out_ref, c_sc, n_sc, m_sc,
                *, n_heads):
    chunk = q_ref.shape[0]
    dh = q_ref.shape[1] // n_heads

    @pl.when(pl.program_id(0) == 0)
    def _():
        c_sc[...] = jnp.zeros_like(c_sc)
        n_sc[...] = jnp.zeros_like(n_sc)
        m_sc[...] = jnp.zeros_like(m_sc)

    r = lax.broadcasted_iota(jnp.int32, (chunk, chunk), 0)
    c = lax.broadcasted_iota(jnp.int32, (chunk, chunk), 1)
    causal = c <= r
    gc = gc_ref[...]
    gr = gr_ref[...]
    for hd in range(n_heads):
        sl = slice(hd * dh, (hd + 1) * dh)
        q, k, v = q_ref[:, sl], k_ref[:, sl], v_ref[:, sl]
        ig_c, b_c = gc[:, hd:hd + 1], gc[:, n_heads + hd:n_heads + hd + 1]
        ig_r, b_r = gr[hd:hd + 1, :], gr[n_heads + hd:n_heads + hd + 1, :]
        m_prev = m_sc[hd * SUBLANES:hd * SUBLANES + 1, 0:1]
        c_prev = c_sc[hd]
        n_prev = n_sc[hd * SUBLANES:hd * SUBLANES + 1, :]

        inter = b_c + m_prev
        dmat = jnp.where(causal, b_c - b_r + ig_r, -jnp.inf)
        m_t = jnp.maximum(inter, jnp.max(dmat, axis=-1, keepdims=True))
        w_inter = jnp.exp(inter - m_t)
        s = lax.dot_general(q, k, (((1,), (1,)), ((), ())), preferred_element_type=F32)
        s = s * jnp.exp(dmat - m_t)
        num = w_inter * _dot(q, c_prev.astype(BF16)) + _dot(s.astype(BF16), v)
        qn = jnp.sum(q.astype(F32) * n_prev, axis=-1, keepdims=True)
        den = w_inter * qn + jnp.sum(s, axis=-1, keepdims=True)
        ht = num / jnp.maximum(jnp.abs(den), jnp.exp(-m_t))
        ht = _rms(ht, mh_ref[:, sl])
        out_ref[:, sl] = (o_ref[:, sl].astype(F32) * ht).astype(out_ref.dtype)

        m_new = m_t[chunk - 1:chunk, :]
        b_last = b_c[chunk - 1:chunk, :]
        decay = jnp.exp(b_last + m_prev - m_new)
        kw = k.astype(F32) * jnp.exp(b_last - b_c + ig_c - m_new)
        c_sc[hd] = decay * c_prev + lax.dot_general(
            kw.astype(BF16), v, (((0,), (0,)), ((), ())), preferred_element_type=F32)
        n_sc[hd * SUBLANES:(hd + 1) * SUBLANES, :] = jnp.broadcast_to(
            decay * n_prev + jnp.sum(kw, axis=0, keepdims=True), (SUBLANES, dh))
        m_sc[hd * SUBLANES:(hd + 1) * SUBLANES, :] = jnp.broadcast_to(m_new, (SUBLANES, LANES))


def _mlstm(qk, v, o, gc, gr, mh, n_heads, chunk):
    s, d_c = v.shape
    dh = d_c // n_heads
    blk = pl.BlockSpec((chunk, d_c), lambda c: (c, 0))
    return pl.pallas_call(
        functools.partial(_mlstm_body, n_heads=n_heads),
        grid=(s // chunk,),
        in_specs=[
            blk,
            pl.BlockSpec((chunk, d_c), lambda c: (c, 1)),
            blk, blk,
            pl.BlockSpec((chunk, LANES), lambda c: (c, 0)),
            pl.BlockSpec((SUBLANES, chunk), lambda c: (0, c)),
            pl.BlockSpec((1, d_c), lambda c: (0, 0)),
        ],
        out_specs=blk,
        out_shape=jax.ShapeDtypeStruct((s, d_c), BF16),
        scratch_shapes=[
            pltpu.VMEM((n_heads, dh, dh), F32),
            pltpu.VMEM((n_heads * SUBLANES, dh), F32),
            pltpu.VMEM((n_heads * SUBLANES, LANES), F32),
        ],
        compiler_params=_params(1),
        name="mlstm",
    )(qk, qk, v, o, gc, gr, mh)


def _mix_body(pa_ref, pb_ref, pc_ref, h_ref, wa_ref, wb_ref, wc_ref, wga_ref, wgb_ref, wgc_ref,
              b_ref, o_ref):
    h = h_ref[...]
    acc = jax.nn.sigmoid(_dot(h, wga_ref[...]) + b_ref[0:1, :]) * _dot(pa_ref[...], wa_ref[...])
    acc += jax.nn.sigmoid(_dot(h, wgb_ref[...]) + b_ref[1:2, :]) * _dot(pb_ref[...], wb_ref[...])
    acc += jax.nn.sigmoid(_dot(h, wgc_ref[...]) + b_ref[2:3, :]) * _dot(pc_ref[...], wc_ref[...])
    o_ref[...] = acc.astype(o_ref.dtype)


def _mix(pa, pb, pc, h, wa, wb, wc, wg, bg, layer):
    s, d = h.shape
    tm = _pick(s, 512, 256, 128)
    tn = _pick(d, 512, 256, 128)
    nj = d // tn
    rows = lambda n: pl.BlockSpec((tm, n), lambda j, i: (i, 0))
    cols = lambda kdim, off: pl.BlockSpec((None, kdim, tn), lambda j, i, off=off: (layer, 0, off + j))
    return pl.pallas_call(
        _mix_body,
        grid=(nj, s // tm),
        in_specs=[
            rows(pa.shape[1]), rows(pb.shape[1]), rows(pc.shape[1]), rows(d),
            cols(wa.shape[1], 0), cols(wb.shape[1], 0), cols(wc.shape[1], 0),
            cols(d, 0), cols(d, nj), cols(d, 2 * nj),
            cols(bg.shape[1], 0),
        ],
        out_specs=pl.BlockSpec((tm, tn), lambda j, i: (i, j)),
        out_shape=jax.ShapeDtypeStruct((s, d), BF16),
        compiler_params=_params(2),
        name="mix",
    )(pa, pb, pc, h, wa, wb, wc, wg, wg, wg, bg)


def _out_body(x_ref, m_ref, w_ref, o_ref):
    o_ref[...] = x_ref[...] + _dot(m_ref[...], w_ref[...])


def _out(x, mix, w_o, layer):
    s, d = x.shape
    tm = _pick(s, 512, 256, 128)
    row = pl.BlockSpec((tm, d), lambda i: (i, 0))
    return pl.pallas_call(
        _out_body,
        grid=(s // tm,),
        in_specs=[row, row, pl.BlockSpec((None, d, d), lambda i: (layer, 0, 0))],
        out_specs=row,
        out_shape=jax.ShapeDtypeStruct((s, d), F32),
        compiler_params=_params(1),
        name="out_proj",
    )(x, mix, w_o)


def _cast_w1_body(*refs, n_valid):
    ins, o_ref = refs[:-1], refs[-1]
    c = pl.program_id(1)
    for q, x_ref in enumerate(ins):
        valid = c * len(ins) + q < n_valid
        o_ref[:, q * LANES:(q + 1) * LANES] = jnp.where(valid, x_ref[...], 0.0).astype(o_ref.dtype)


def _cast_w2_body(*refs, n_valid):
    ins, o_ref = refs[:-1], refs[-1]
    c = pl.program_id(0)
    for q, x_ref in enumerate(ins):
        valid = c * len(ins) + q < n_valid
        o_ref[q * LANES:(q + 1) * LANES, :] = jnp.where(valid, x_ref[...], 0.0).astype(o_ref.dtype)


def _ffn_weights(w1, w2, layer):
    _, d, _ = w1.shape
    d_ff = w2.shape[1]
    assert d_ff % LANES == 0
    nb = d_ff // LANES
    per = FFN_TILE_F // LANES
    fp = -(-d_ff // FFN_TILE_F) * FFN_TILE_F
    w1_specs = [pl.BlockSpec((None, d, LANES),
                             lambda g, c, q=q: (layer, 0, g * nb + jnp.minimum(c * per + q, nb - 1)))
                for q in range(per)]
    w1b = pl.pallas_call(
        functools.partial(_cast_w1_body, n_valid=nb),
        grid=(2, fp // FFN_TILE_F),
        in_specs=w1_specs,
        out_specs=pl.BlockSpec((None, d, FFN_TILE_F), lambda g, c: (g, 0, c)),
        out_shape=jax.ShapeDtypeStruct((2, d, fp), BF16),
        compiler_params=_params(2),
        name="cast_w1",
    )(*([w1] * per))
    w2_specs = [pl.BlockSpec((None, LANES, d), lambda c, q=q: (layer, jnp.minimum(c * per + q, nb - 1), 0))
                for q in range(per)]
    w2b = pl.pallas_call(
        functools.partial(_cast_w2_body, n_valid=nb),
        grid=(fp // FFN_TILE_F,),
        in_specs=w2_specs,
        out_specs=pl.BlockSpec((FFN_TILE_F, d), lambda c: (c, 0)),
        out_shape=jax.ShapeDtypeStruct((fp, d), BF16),
        compiler_params=_params(1),
        name="cast_w2",
    )(*([w2] * per))
    return w1b, w2b


def kernel(x, ffn1_norm, ffn1_w1, ffn1_w2, mix_norm, w_in, conv_a_w, w_out_a, conv_b_w, conv_b_bias,
           ln_b_gain, ln_b_bias, w_out_b, conv_c_w, ig_bias, fg_bias, mh_norm, w_out_c, w_gate, b_gate,
           w_o, ffn2_norm, ffn2_w1, ffn2_w2, final_norm):
    bsz, s, d = x.shape
    depth = ffn1_norm.shape[0]
    d_a, d_b, d_c = conv_a_w.shape[-1], conv_b_w.shape[-1], mh_norm.shape[-1]
    n_heads = ig_bias.shape[-1]
    chunk = _pick(s, MLSTM_CHUNK)
    col_b = 3 * d_a
    col_q = col_b + 2 * d_b
    col_v = col_q + 2 * d_c
    col_o = col_v + d_c
    col_g = col_o + d_c
    vec = lambda a: a.reshape(1, -1).astype(F32)
    qk_scale = jnp.concatenate(
        [jnp.full((1, d_c), (d_c // n_heads) ** -0.5, F32), jnp.ones((1, d_c), F32)], axis=-1)

    side_ok = _ffn_can_side(s, ffn1_w2.shape[1])
    w_in_b = w_in.astype(BF16)
    wa_b, wb_b, wc_b = w_out_a.astype(BF16), w_out_b.astype(BF16), w_out_c.astype(BF16)
    wg_b, wo_b = w_gate.astype(BF16), w_o.astype(BF16)
    bg3 = b_gate.reshape(depth, 3, d).astype(F32)
    gate_pad = ((0, 0), (0, LANES - 2 * n_heads))

    outs = []
    for b in range(bsz):
        xb = x[b]
        f1w = _ffn_weights(ffn1_w1, ffn1_w2, 0)
        for l in range(depth):
            res = _ffn(xb, vec(ffn1_norm[l]), *f1w, vec(mix_norm[l]), emit_x=True, emit_h=True, h_dtype=BF16,
                       side=(ffn2_w1, ffn2_w2, l) if side_ok else None)
            xb, h = res[0], res[1]
            f2w = res[2:] if side_ok else _ffn_weights(ffn2_w1, ffn2_w2, l)

            pa = _proj_a(h, w_in_b, l, conv_a_w, d_a)
            pb = _proj_b(h, w_in_b, l, conv_b_w, vec(conv_b_bias[l]), vec(ln_b_gain[l]),
                         vec(ln_b_bias[l]), col_b)
            qk = _proj_qk(h, w_in_b, l, col_q, conv_c_w, qk_scale)
            v = _proj_plain(h, w_in_b, l, col_v, d_c, gate=False, name="proj_v")
            o = _proj_plain(h, w_in_b, l, col_o, d_c, gate=True, name="proj_o")
            w_g = jnp.pad(w_in_b[l, :, col_g:], gate_pad)
            b_g = jnp.pad(jnp.concatenate([ig_bias[l], fg_bias[l]]).reshape(1, -1).astype(F32), gate_pad)
            gc, gr = _gates(h, w_g, b_g, n_heads, chunk)
            pc = _mlstm(qk, v, o, gc, gr, vec(mh_norm[l]), n_heads, chunk)

            mix = _mix(pa, pb, pc, h, wa_b, wb_b, wc_b, wg_b, bg3, l)
            xb = _out(xb, mix, wo_b, l)

            last = l == depth - 1
            res = _ffn(xb, vec(ffn2_norm[l]), *f2w, vec(final_norm), emit_x=not last, emit_h=last, h_dtype=F32,
                       side=(ffn1_w1, ffn1_w2, l + 1) if side_ok and not last else None)
            xb = res[0]
            if not last:
                f1w = res[1:] if side_ok else _ffn_weights(ffn1_w1, ffn1_w2, l + 1)
        outs.append(xb)
    return outs[0][None] if bsz == 1 else jnp.stack(outs, axis=0)
```
